```python
import math, functools
import jax, jax.numpy as jnp
from jax import lax
import numpy as np

D_MODEL = 1024
BATCH = 16
SEQ = 2048
DEPTH = 2
DEC_BATCH = 32
DEC_SEQ = 4
PAST_LEN = 16384
PAGE_SIZE = 128

R_HEAD = 64
R_HEADS = 8
R_WIDTH = R_HEADS * R_HEAD
W_RANK = 64
A_RANK = 64
V_RANK = 32
G_RANK = 128
GN_EPS = 64e-5
A_HEADS = 4
A_DK = 64
A_DV = 2 * A_DK
A_QK_WIDTH = A_HEADS * 2 * A_DK
A_V_WIDTH = A_HEADS * A_DV
ROT_DIM = A_DK // 4
ROPE_THETA = 500000.0
Q_BLOCK = 128
D_FF = 4 * D_MODEL
N_BRANCH = 2
NORM_EPS = 1e-6
SHIFT_SPLITS = (R_WIDTH, 2 * R_WIDTH, 3 * R_WIDTH, 3 * R_WIDTH + W_RANK, 3 * R_WIDTH + W_RANK + A_RANK)
SHIFT_COLS = 3 * R_WIDTH + W_RANK + A_RANK + G_RANK
IN_COLS = SHIFT_COLS + 2 * A_QK_WIDTH + A_V_WIDTH + N_BRANCH * D_MODEL

kernel_name = "hybrid_rwkv7_diffattn_adaln_step"


def rms_norm(x, gain, eps=NORM_EPS):
    xf = x.astype(jnp.float32)
    y = xf * lax.rsqrt(jnp.mean(xf * xf, axis=-1, keepdims=True) + eps)
    return (y * gain.astype(jnp.float32)).astype(x.dtype)


def rope_partial(x, pos):
    half = ROT_DIM // 2
    inv = ROPE_THETA ** (-jnp.arange(half, dtype=jnp.float32) / half)
    ang = pos.astype(jnp.float32)[:, None] * inv[None, :]
    cos = jnp.cos(ang)[:, None, None, :]
    sin = jnp.sin(ang)[:, None, None, :]
    x1 = x[..., :half].astype(jnp.float32)
    x2 = x[..., half:ROT_DIM].astype(jnp.float32)
    rot = jnp.concatenate([x1 * cos - x2 * sin, x2 * cos + x1 * sin], axis=-1).astype(x.dtype)
    return jnp.concatenate([rot, x[..., ROT_DIM:]], axis=-1)


def diff_attend(q, k, v, q_pos, k_pos, lam):
    s = jnp.einsum('bqhcd,bkhcd->bchqk', q, k).astype(jnp.float32) * (A_DK ** -0.5)
    mask = k_pos[None, :] <= q_pos[:, None]
    s = jnp.where(mask, s, -jnp.inf)
    p = jax.nn.softmax(s, axis=-1)
    a = p[:, 0] - lam * p[:, 1]
    return jnp.einsum('bhqk,bkhd->bqhd', a.astype(v.dtype), v)


def diff_attn_prompt(q, k, v, lam):
    B, T = q.shape[0], q.shape[1]
    pos = jnp.arange(T)

    def block(i):
        start = i * Q_BLOCK
        qb = lax.dynamic_slice_in_dim(q, start, Q_BLOCK, axis=1)
        qp = lax.dynamic_slice_in_dim(pos, start, Q_BLOCK)
        return diff_attend(qb, k, v, qp, pos, lam)

    out = lax.map(block, jnp.arange(T // Q_BLOCK))
    return jnp.moveaxis(out, 0, 1).reshape(B, T, A_HEADS, A_DV)


def diff_attn_cached(q, k, v, lam, past_k, past_v):
    T = q.shape[1]
    P = past_k.shape[1]
    kf = jnp.concatenate([past_k.astype(k.dtype), k], axis=1)
    vf = jnp.concatenate([past_v.astype(v.dtype), v], axis=1)
    q_pos = P + jnp.arange(T)
    k_pos = jnp.arange(P + T)
    return diff_attend(q, kf, vf, q_pos, k_pos, lam)


def wkv_scan(S0, r, w, k, v, kk, a):
    def step(S, inp):
        r_t, w_t, k_t, v_t, kk_t, a_t = inp
        sa = jnp.einsum('bhvk,bhk->bhv', S, -kk_t)
        S = S * w_t[:, :, None, :] + sa[..., None] * (kk_t * a_t)[:, :, None, :] + v_t[..., None] * k_t[:, :, None, :]
        y = jnp.einsum('bhvk,bhk->bhv', S, r_t)
        return S, y

    xs = tuple(jnp.moveaxis(t, 1, 0) for t in (r, w, k, v, kk, a))
    S, ys = lax.scan(step, S0, xs)
    return S, jnp.moveaxis(ys, 0, 1)


def rwkv_branch(zs, shift_prev, S0, v_first, l, p):
    B, T, _ = zs.shape
    f32 = jnp.float32
    prev = jnp.concatenate([shift_prev[:, None, :].astype(zs.dtype), zs[:, :-1]], axis=1)
    xs = zs + (prev - zs) * p['shift_mu'][l]
    r, k, v, zw, za, zg = jnp.split(xs, SHIFT_SPLITS, axis=-1)
    w_log = -jax.nn.softplus(-(p['rw_w0'][l] + jnp.tanh(zw) @ p['rw_wb'][l]).astype(f32)) - 0.5
    decay = jnp.exp(-jnp.exp(w_log))
    a = jax.nn.sigmoid((p['rw_a0'][l] + za @ p['rw_ab'][l]).astype(f32))
    g = jax.nn.sigmoid(zg) @ p['rw_gb'][l]
    if l == 0:
        v_first = v
    else:
        vg = jax.nn.sigmoid(p['rw_v0'][l - 1] + (v @ p['rw_va'][l - 1]) @ p['rw_vb'][l - 1])
        v = v + (v_first - v) * vg
    heads = lambda t: t.reshape(B, T, R_HEADS, R_HEAD)
    kk = heads((k * p['rw_kk'][l]).astype(f32))
    kk = kk / jnp.maximum(jnp.sqrt(jnp.sum(kk * kk, axis=-1, keepdims=True)), 1e-12)
    k_mod = k.astype(f32) * (1.0 + (a - 1.0) * p['rw_ka'][l].astype(f32))
    rh, kh, vh, ah, dh = heads(r.astype(f32)), heads(k_mod), heads(v.astype(f32)), heads(a), heads(decay)
    S, y = wkv_scan(S0.astype(f32), rh, dh, kh, vh, kk, ah)
    mu = jnp.mean(y, axis=-1, keepdims=True)
    var = jnp.mean(jnp.square(y - mu), axis=-1, keepdims=True)
    y = ((y - mu) * lax.rsqrt(var + GN_EPS)).reshape(B, T, R_WIDTH)
    y = y * p['rw_ln_w'][l].astype(f32) + p['rw_ln_b'][l].astype(f32)
    bonus = jnp.sum(rh * kh * p['rw_rk'][l].astype(f32), axis=-1, keepdims=True) * vh
    y = (y + bonus.reshape(B, T, R_WIDTH)) * g.astype(f32)
    return y.astype(zs.dtype), S.astype(S0.dtype), zs[:, -1], v_first


def layer(x, c, l, pos, shift_prev, S0, v_first, attend_fn, p):
    B, T, D = x.shape
    mod = jax.nn.silu(c) @ p['ada_w'][l] + p['ada_b'][l]
    sh1, sc1, gt1, sh2, sc2, gt2 = [m[:, None, :] for m in jnp.split(mod, 6, axis=-1)]
    h = rms_norm(x, p['norm1_g'][l]) * (1 + sc1) + sh1
    z = h @ p['w_in'][l]
    o = SHIFT_COLS
    zs = z[..., :o]
    q = z[..., o:o + A_QK_WIDTH].reshape(B, T, A_HEADS, 2, A_DK); o += A_QK_WIDTH
    k = z[..., o:o + A_QK_WIDTH].reshape(B, T, A_HEADS, 2, A_DK); o += A_QK_WIDTH
    v = z[..., o:o + A_V_WIDTH].reshape(B, T, A_HEADS, A_DV); o += A_V_WIDTH
    gates = jax.nn.sigmoid(z[..., o:]).reshape(B, T, N_BRANCH, D)
    q = rope_partial(rms_norm(q, p['at_qnorm'][l]), pos)
    k = rope_partial(rms_norm(k, p['at_knorm'][l]), pos)
    lam_init = 0.8 - 0.6 * math.exp(-0.3 * l)
    lam = (jnp.exp(jnp.sum(p['at_lq1'][l] * p['at_lk1'][l]).astype(jnp.float32))
           - jnp.exp(jnp.sum(p['at_lq2'][l] * p['at_lk2'][l]).astype(jnp.float32)) + lam_init)
    att = attend_fn(q, k, v, lam)
    att = rms_norm(att, p['at_subln'][l]) * (1.0 - lam_init)
    rw, S, sh_new, v_first = rwkv_branch(zs, shift_prev, S0, v_first, l, p)
    merged = (gates[:, :, 0] * (rw @ p['br_rwkv'][l])
              + gates[:, :, 1] * (att.reshape(B, T, A_V_WIDTH) @ p['br_attn'][l]))
    x = x + gt1 * (merged @ p['w_out'][l])
    h2 = rms_norm(x, p['norm2_g'][l]) * (1 + sc2) + sh2
    x = x + gt2 * (jnp.square(jax.nn.relu(h2 @ p['mlp_w1'][l])) @ p['mlp_w2'][l])
    return x, k, v, S, sh_new, v_first


def setup_inputs(seed: int = 0) -> dict:
    key = jax.random.key(seed)
    ks = iter(jax.random.split(key, 48))
    nrm = lambda shape, scale: jax.random.normal(next(ks), shape, jnp.float32) * scale
    n_pages = PAST_LEN // PAGE_SIZE
    n_phys = (DEC_BATCH * n_pages * 5) // 4
    D = D_MODEL
    inp = {}
    inp['x_prompt'] = nrm((BATCH, SEQ, D), 1.0)
    inp['x_sample'] = nrm((DEC_BATCH, DEC_SEQ, D), 1.0)
    inp['c_prompt'] = nrm((BATCH, D), 1.0)
    inp['c_sample'] = nrm((DEC_BATCH, D), 1.0)
    inp['cache_k'] = nrm((DEPTH, n_phys, PAGE_SIZE, A_HEADS, 2, A_DK), 1.0)
    inp['cache_v'] = nrm((DEPTH, n_phys, PAGE_SIZE, A_HEADS, A_DV), 1.0)
    inp['state_wkv'] = nrm((DEPTH, DEC_BATCH, R_HEADS, R_HEAD, R_HEAD), 0.3)
    inp['state_shift'] = nrm((DEPTH, DEC_BATCH, SHIFT_COLS), 1.0)
    perm = jax.random.permutation(next(ks), n_phys)
    inp['page_table'] = perm[:DEC_BATCH * n_pages].reshape(DEC_BATCH, n_pages).astype(jnp.int32)
    inp['norm1_g'] = 1.0 + nrm((DEPTH, D), 0.05)
    inp['norm2_g'] = 1.0 + nrm((DEPTH, D), 0.05)
    inp['ada_w'] = nrm((DEPTH, D, 6 * D), 0.5 * D ** -0.5)
    inp['ada_b'] = nrm((DEPTH, 6 * D), 0.01)
    inp['w_in'] = nrm((DEPTH, D, IN_COLS), D ** -0.5)
    inp['shift_mu'] = jax.random.uniform(next(ks), (DEPTH, SHIFT_COLS), jnp.float32)
    inp['rw_w0'] = -2.0 + nrm((DEPTH, R_WIDTH), 0.5)
    inp['rw_wb'] = nrm((DEPTH, W_RANK, R_WIDTH), 0.1 * W_RANK ** -0.5)
    inp['rw_a0'] = nrm((DEPTH, R_WIDTH), 0.1)
    inp['rw_ab'] = nrm((DEPTH, A_RANK, R_WIDTH), 0.1 * A_RANK ** -0.5)
    inp['rw_gb'] = nrm((DEPTH, G_RANK, R_WIDTH), G_RANK ** -0.5)
    inp['rw_v0'] = nrm((DEPTH - 1, R_WIDTH), 0.1)
    inp['rw_va'] = nrm((DEPTH - 1, R_WIDTH, V_RANK), R_WIDTH ** -0.5)
    inp['rw_vb'] = nrm((DEPTH - 1, V_RANK, R_WIDTH), 0.5 * V_RANK ** -0.5)
    inp['rw_kk'] = 0.85 + nrm((DEPTH, R_WIDTH), 0.05)
    inp['rw_ka'] = 1.0 + nrm((DEPTH, R_WIDTH), 0.05)
    inp['rw_rk'] = nrm((DEPTH, R_HEADS, R_HEAD), 0.1)
    inp['rw_ln_w'] = 1.0 + nrm((DEPTH, R_WIDTH), 0.05)
    inp['rw_ln_b'] = nrm((DEPTH, R_WIDTH), 0.01)
    inp['at_qnorm'] = 1.0 + nrm((DEPTH, A_DK), 0.05)
    inp['at_knorm'] = 1.0 + nrm((DEPTH, A_DK), 0.05)
    inp['at_lq1'] = nrm((DEPTH, A_DK), 0.1)
    inp['at_lk1'] = nrm((DEPTH, A_DK), 0.1)
    inp['at_lq2'] = nrm((DEPTH, A_DK), 0.1)
    inp['at_lk2'] = nrm((DEPTH, A_DK), 0.1)
    inp['at_subln'] = 1.0 + nrm((DEPTH, A_DV), 0.05)
    inp['br_rwkv'] = nrm((DEPTH, R_WIDTH, D), R_WIDTH ** -0.5)
    inp['br_attn'] = nrm((DEPTH, A_V_WIDTH, D), A_V_WIDTH ** -0.5)
    inp['w_out'] = nrm((DEPTH, D, D), D ** -0.5)
    inp['mlp_w1'] = nrm((DEPTH, D, D_FF), D ** -0.5)
    inp['mlp_w2'] = nrm((DEPTH, D_FF, D), D_FF ** -0.5)
    return inp


def reference(x_prompt, x_sample, c_prompt, c_sample, cache_k, cache_v, state_wkv, state_shift, page_table,
              norm1_g, norm2_g, ada_w, ada_b, w_in, shift_mu, rw_w0, rw_wb, rw_a0, rw_ab, rw_gb,
              rw_v0, rw_va, rw_vb, rw_kk, rw_ka, rw_rk, rw_ln_w, rw_ln_b, at_qnorm, at_knorm,
              at_lq1, at_lk1, at_lq2, at_lk2, at_subln, br_rwkv, br_attn, w_out, mlp_w1, mlp_w2):
    p = dict(norm1_g=norm1_g, norm2_g=norm2_g, ada_w=ada_w, ada_b=ada_b, w_in=w_in, shift_mu=shift_mu,
             rw_w0=rw_w0, rw_wb=rw_wb, rw_a0=rw_a0, rw_ab=rw_ab, rw_gb=rw_gb, rw_v0=rw_v0, rw_va=rw_va,
             rw_vb=rw_vb, rw_kk=rw_kk, rw_ka=rw_ka, rw_rk=rw_rk, rw_ln_w=rw_ln_w, rw_ln_b=rw_ln_b,
             at_qnorm=at_qnorm, at_knorm=at_knorm, at_lq1=at_lq1, at_lk1=at_lk1, at_lq2=at_lq2,
             at_lk2=at_lk2, at_subln=at_subln, br_rwkv=br_rwkv, br_attn=br_attn, w_out=w_out,
             mlp_w1=mlp_w1, mlp_w2=mlp_w2)
    Bp, Tp, _ = x_prompt.shape
    Bs, Ts, _ = x_sample.shape
    n_pages = PAST_LEN // PAGE_SIZE
    pos_p = jnp.arange(Tp)
    pos_s = PAST_LEN + jnp.arange(Ts)
    xp, xs = x_prompt, x_sample
    vfp = vfs = None
    kp_l, vp_l, ks_l, vs_l, Sp_l, Ss_l, shp_l, shs_l = [], [], [], [], [], [], [], []
    for l in range(DEPTH):
        xp, kp, vp, Sp, shp, vfp = layer(
            xp, c_prompt, l, pos_p,
            jnp.zeros((Bp, SHIFT_COLS), x_prompt.dtype),
            jnp.zeros((Bp, R_HEADS, R_HEAD, R_HEAD), x_prompt.dtype),
            vfp, diff_attn_prompt, p)
        past_k = cache_k[l][page_table].reshape(Bs, n_pages * PAGE_SIZE, A_HEADS, 2, A_DK)
        past_v = cache_v[l][page_table].reshape(Bs, n_pages * PAGE_SIZE, A_HEADS, A_DV)
        attend_s = functools.partial(diff_attn_cached, past_k=past_k, past_v=past_v)
        xs, ksm, vsm, Ss, shs, vfs = layer(
            xs, c_sample, l, pos_s, state_shift[l], state_wkv[l], vfs, attend_s, p)
        kp_l.append(kp); vp_l.append(vp); ks_l.append(ksm); vs_l.append(vsm)
        Sp_l.append(Sp); Ss_l.append(Ss); shp_l.append(shp); shs_l.append(shs)
    y_prompt, y_sample = xp, xs
    new_k_prompt = jnp.stack(kp_l)
    new_v_prompt = jnp.stack(vp_l)
    new_k_sample = jnp.stack(ks_l)
    new_v_sample = jnp.stack(vs_l)
    new_wkv_prompt = jnp.stack(Sp_l)
    new_wkv_sample = jnp.stack(Ss_l)
    new_shift_prompt = jnp.stack(shp_l)
    new_shift_sample = jnp.stack(shs_l)
    return (y_prompt, y_sample, new_k_prompt, new_v_prompt, new_k_sample, new_v_sample,
            new_wkv_prompt, new_wkv_sample, new_shift_prompt, new_shift_sample)
```

```python
import functools
import math

import jax
import jax.numpy as jnp
from jax import lax
from jax.experimental import pallas as pl
from jax.experimental.pallas import tpu as pltpu

F32, BF16 = jnp.float32, jnp.bfloat16

R_HEAD = 64
GN_EPS = 64e-5
A_HEADS = 4
A_DK = 64
ROT_DIM = A_DK // 4
ROPE_THETA = 500000.0
NORM_EPS = 1e-6
PAGE_SIZE = 128
W_RANK, A_RANK, G_RANK = 64, 64, 128

LANES = 128
SUBLANES = 8
VMEM_LIMIT = 56 * 1024 * 1024

CHUNK = 64
PAGES_PER_STEP = 8

NN = (((1,), (0,)), ((), ()))
NT = (((1,), (1,)), ((), ()))


def _cparams(*sem):
    return pltpu.CompilerParams(dimension_semantics=sem, vmem_limit_bytes=VMEM_LIMIT)


def _dot(a, b, dims=NN):
    return lax.dot_general(a.astype(BF16), b.astype(BF16), dims, preferred_element_type=F32)


def _split2(x):
    hi = x.astype(BF16)
    lo = (x - hi.astype(F32)).astype(BF16)
    return hi, lo


def _mm3(a, b, dims=NN):
    d = lambda x, y: lax.dot_general(x, y, dims, preferred_element_type=F32)
    return d(a[0], b[0]) + d(a[0], b[1]) + d(a[1], b[0])


def _seg(x, e):
    hi, lo = _split2(x)
    d = lambda u: lax.dot_general(u, e, NN, preferred_element_type=F32)
    return d(hi) + d(lo)


def _sigmoid(x):
    return 1.0 / (1.0 + jnp.exp(-x))


def _rms_mod(x, g, sc, sh):
    y = x * lax.rsqrt(jnp.mean(x * x, axis=-1, keepdims=True) + NORM_EPS) * g
    return y * (1.0 + sc) + sh


def _ada_kernel(c_ref, w_ref, b_ref, o_ref):
    c = c_ref[...]
    o_ref[0] = _dot(c * _sigmoid(c), w_ref[0]) + b_ref[0]


def _ada_mod(c_all, ada_w, ada_b):
    depth, d, n = ada_w.shape
    rows = c_all.shape[0]
    tn = d
    return pl.pallas_call(
        _ada_kernel,
        grid=(depth, n // tn),
        in_specs=[
            pl.BlockSpec((rows, d), lambda l, j: (0, 0)),
            pl.BlockSpec((1, d, tn), lambda l, j: (l, 0, j)),
            pl.BlockSpec((1, 1, tn), lambda l, j: (l, 0, j)),
        ],
        out_specs=pl.BlockSpec((1, rows, tn), lambda l, j: (l, 0, j)),
        out_shape=jax.ShapeDtypeStruct((depth, rows, n), F32),
        compiler_params=_cparams("parallel", "parallel"),
        name="ada_mod",
    )(c_all, ada_w, ada_b.reshape(depth, 1, n))


def _inproj_kernel(shift_cols, qk_w, v_w, x_ref, sc_ref, sh_ref, g_ref, w_ref, qn_ref, kn_ref, e_ref,
                   cos_ref, s1_ref, s2_ref, zs_ref, q_ref, k_ref, kb_ref, v_ref, vb_ref, gate_ref):
    hb = _rms_mod(x_ref[...], g_ref[...], sc_ref[0], sh_ref[0]).astype(BF16)
    o = 0
    zs_ref[...] = jnp.dot(hb, w_ref[:, o:o + shift_cols], preferred_element_type=F32)
    o += shift_cols
    reps = qk_w // LANES
    cos = jnp.concatenate([cos_ref[...]] * reps, axis=1)
    s1 = jnp.concatenate([s1_ref[...]] * reps, axis=1)
    s2 = jnp.concatenate([s2_ref[...]] * reps, axis=1)

    def norm_rope(z, gain):
        ms = _seg(z * z, e_ref[...]) * (1.0 / A_DK)
        y = z * lax.rsqrt(ms + NORM_EPS) * gain
        return y * cos + pltpu.roll(y, ROT_DIM // 2, 1) * s1 + pltpu.roll(y, qk_w - ROT_DIM // 2, 1) * s2

    q = norm_rope(jnp.dot(hb, w_ref[:, o:o + qk_w], preferred_element_type=F32), qn_ref[...])
    q_ref[...] = (q * (A_DK ** -0.5)).astype(BF16)
    o += qk_w
    k = norm_rope(jnp.dot(hb, w_ref[:, o:o + qk_w], preferred_element_type=F32), kn_ref[...])
    k_ref[...] = k
    kb_ref[...] = k.astype(BF16)
    o += qk_w
    v = jnp.dot(hb, w_ref[:, o:o + v_w], preferred_element_type=F32)
    v_ref[...] = v
    vb_ref[...] = v.astype(BF16)
    o += v_w
    gate_ref[...] = _sigmoid(jnp.dot(hb, w_ref[:, o:], preferred_element_type=F32))


def _inproj(x, mods, g, w_bf, qn, kn, e_qk, rope, shift_cols, qk_w, v_w, tm, tps):
    rows, d = x.shape
    in_cols = w_bf.shape[1]
    gate_w = in_cols - shift_cols - 2 * qk_w - v_w
    sc, sh = mods
    r_mod = sc.shape[1]
    row = lambda i: (i, 0)
    const = lambda i: (0, 0)
    modmap = lambda i: (i // tps, 0, 0)
    ropemap = lambda i: (i % tps, 0)
    outs = [(shift_cols, F32), (qk_w, BF16), (qk_w, F32), (qk_w, BF16), (v_w, F32), (v_w, BF16), (gate_w, F32)]
    return pl.pallas_call(
        functools.partial(_inproj_kernel, shift_cols, qk_w, v_w),
        grid=(rows // tm,),
        in_specs=[
            pl.BlockSpec((tm, d), row),
            pl.BlockSpec((1, r_mod, d), modmap),
            pl.BlockSpec((1, r_mod, d), modmap),
            pl.BlockSpec((1, d), const),
            pl.BlockSpec((d, in_cols), const),
            pl.BlockSpec((1, qk_w), const),
            pl.BlockSpec((1, qk_w), const),
            pl.BlockSpec((qk_w, qk_w), const),
            pl.BlockSpec((tm, LANES), ropemap),
            pl.BlockSpec((tm, LANES), ropemap),
            pl.BlockSpec((tm, LANES), ropemap),
        ],
        out_specs=[pl.BlockSpec((tm, w), row) for w, _ in outs],
        out_shape=[jax.ShapeDtypeStruct((rows, w), dt) for w, dt in outs],
        compiler_params=_cparams("parallel"),
        name="inproj",
    )(x, sc, sh, g, w_bf, qn, kn, e_qk, *rope)


def _prep_kernel(halo, has_vres, tps, rw, *refs):
    if has_vres:
        (zs_ref, prev_ref, first_ref, mu_ref, w0_ref, wb_ref, a0_ref, ab_ref, gb_ref, kk_ref, ka_ref, e_ref,
         v0_ref, va_ref, vb_ref, vf_ref, r_o, lw_o, k_o, v_o, kp_o, b_o, g_o) = refs
    else:
        (zs_ref, prev_ref, first_ref, mu_ref, w0_ref, wb_ref, a0_ref, ab_ref, gb_ref, kk_ref, ka_ref, e_ref,
         r_o, lw_o, k_o, v_o, kp_o, b_o, g_o) = refs
    z = zs_ref[...]
    if halo:
        tm = z.shape[0]
        rolled = pltpu.roll(z, 1, 0)
        first = jnp.where(pl.program_id(0) % tps == 0, first_ref[0], prev_ref[SUBLANES - 1:SUBLANES, :])
        prev = jnp.where(lax.broadcasted_iota(jnp.int32, (tm, 1), 0) == 0, first, rolled)
    else:
        prev = prev_ref[...]
    xs = z + (prev - z) * mu_ref[...]
    r = xs[:, 0:rw]
    k = xs[:, rw:2 * rw]
    v = xs[:, 2 * rw:3 * rw]
    zwa = xs[:, 3 * rw:3 * rw + W_RANK + A_RANK]
    zg = xs[:, 3 * rw + W_RANK + A_RANK:]
    u = -(w0_ref[...] + _dot(jnp.tanh(zwa), wb_ref[...]))
    softplus = jnp.maximum(u, 0.0) + jnp.log(1.0 + jnp.exp(-jnp.abs(u)))
    lw_o[...] = -jnp.exp(-softplus - 0.5)
    a = _sigmoid(a0_ref[...] + _dot(zwa, ab_ref[...]))
    g_o[...] = _dot(_sigmoid(zg), gb_ref[...])
    if has_vres:
        vg = _sigmoid(v0_ref[...] + _dot(_dot(v, va_ref[...]), vb_ref[...]))
        v = v + (vf_ref[...] - v) * vg
    kk = k * kk_ref[...]
    nrm = jnp.sqrt(_seg(kk * kk, e_ref[...]))
    kk = kk / jnp.maximum(nrm, 1e-12)
    r_o[...] = r
    k_o[...] = k * (1.0 + (a - 1.0) * ka_ref[...])
    v_o[...] = v
    kp_o[...] = kk
    b_o[...] = kk * a


def _rwkv_prep(zs, prev_src, first, p, e_rw, v_first, halo, tm, tps):
    rows, cols = zs.shape
    rw = p["rw_w0"].shape[1]
    has_vres = v_first is not None
    row = lambda i: (i, 0)
    const = lambda i: (0, 0)
    full = lambda a: pl.BlockSpec(a.shape, const)
    if halo:
        prev_spec = pl.BlockSpec((SUBLANES, cols), lambda i: (jnp.maximum(i * (tm // SUBLANES) - 1, 0), 0))
        first_spec = pl.BlockSpec((1, 1, cols), lambda i: (i // tps, 0, 0))
    else:
        prev_spec = pl.BlockSpec((tm, cols), row)
        first_spec = pl.BlockSpec((1, 1, cols), lambda i: (0, 0, 0))
    params = [p["shift_mu"], p["rw_w0"], p["wb_pad"], p["rw_a0"], p["ab_pad"], p["rw_gb"], p["rw_kk"], p["rw_ka"],
              e_rw]
    args = [zs, prev_src, first] + params
    in_specs = [pl.BlockSpec((tm, cols), row), prev_spec, first_spec] + [full(a) for a in params]
    if has_vres:
        extra = [p["rw_v0"], p["rw_va"], p["rw_vb"]]
        args += extra + [v_first]
        in_specs += [full(a) for a in extra] + [pl.BlockSpec((tm, rw), row)]
    return pl.pallas_call(
        functools.partial(_prep_kernel, halo, has_vres, tps, rw),
        grid=(rows // tm,),
        in_specs=in_specs,
        out_specs=[pl.BlockSpec((tm, rw), row)] * 7,
        out_shape=[jax.ShapeDtypeStruct((rows, rw), F32)] * 7,
        compiler_params=_cparams("parallel"),
        name="rwkv_prep",
    )(*args)


def _wkv_kernel(r_ref, lw_ref, k_ref, v_ref, kp_ref, b_ref, g_ref, rk_ref, lnw_ref, lnb_ref, z0_ref, y_ref, z_ref):
    C = CHUNK
    P2 = 2 * C

    @pl.when(pl.program_id(1) == 0)
    def _():
        z_ref[...] = z0_ref[...]

    ri = lax.broadcasted_iota(jnp.int32, (P2, P2), 0)
    ci = lax.broadcasted_iota(jnp.int32, (P2, P2), 1)
    strict = ci < ri
    incl = ci <= ri
    eye = (ci == ri).astype(F32)
    blk = lambda s: (ri >> s) == (ci >> s)
    e_head = blk(6).astype(BF16)
    ti = lax.broadcasted_iota(jnp.int32, (C, C), 0)
    tj = lax.broadcasted_iota(jnp.int32, (C, C), 1)
    tril1 = (tj <= ti).astype(BF16)
    head_a = lax.broadcasted_iota(jnp.int32, (C, LANES), 1) < R_HEAD

    def stack(x):
        return jnp.concatenate([jnp.where(head_a, x, 0.0), jnp.where(head_a, 0.0, x)], axis=0)

    n_pairs = r_ref.shape[2] // LANES
    for p in range(n_pairs):
        sl = slice(p * LANES, (p + 1) * LANES)
        r, lw, k, v, kp, b = (ref[0, :, sl] for ref in (r_ref, lw_ref, k_ref, v_ref, kp_ref, b_ref))
        l0 = lw.astype(BF16)
        l1 = (lw - l0.astype(F32))
        l1b = l1.astype(BF16)
        l2b = (l1 - l1b.astype(F32)).astype(BF16)
        cs = lambda u: jnp.dot(tril1, u, preferred_element_type=F32)
        logp = cs(l0) + cs(l1b) + cs(l2b)
        logpc = logp[C - 1:C, :]
        e_neg = jnp.exp(-logp)
        e_end = jnp.exp(logpc - logp)
        rt = stack(r * jnp.exp(logp))
        kpt = stack(kp * jnp.exp(logp - lw))
        v_s = stack(v)
        rt2, kpt2, v2 = _split2(rt), _split2(kpt), _split2(v_s)
        bt2 = _split2(stack(b * e_neg))
        kt2 = _split2(stack(k * e_neg))
        lb = jnp.where(strict, _mm3(kpt2, bt2, NT), 0.0)
        lk = jnp.where(strict, _mm3(kpt2, kt2, NT), 0.0)
        mb = jnp.where(incl, _mm3(rt2, bt2, NT), 0.0)
        mk = jnp.where(incl, _mm3(rt2, kt2, NT), 0.0)
        n = jnp.where(blk(3), -lb, 0.0)
        t = eye + n
        n2 = _split2(n)
        n = _mm3(n2, n2)
        t = _mm3(_split2(t), _split2(eye + n))
        n2 = _split2(n)
        n = _mm3(n2, n2)
        t = _mm3(_split2(t), _split2(eye + n))
        for s in (3, 4, 5):
            off = jnp.where(blk(s + 1) & jnp.logical_not(blk(s)), lb, 0.0)
            t2 = _split2(t)
            t = t - _mm3(t2, _split2(_mm3(_split2(off), t2)))
        t2 = _split2(t)
        x = _mm3(_split2(lk), v2)
        tk = _mm3(t2, _split2(jnp.concatenate([kpt, x], axis=1)))
        wu = -tk
        wu2 = _split2(wu)
        mbwu = _mm3(_split2(mb), wu2)
        q = rt + mbwu[:, :LANES]
        y0 = mbwu[:, LANES:] + _mm3(_split2(mk), v2)
        bh_t = _split2(stack(b * e_end).T)
        kh_t = _split2(stack(k * e_end).T)
        bhwu = _mm3(bh_t, wu2)
        a_mat = eye * jnp.exp(logpc) + bhwu[:, :LANES]
        g_mat = bhwu[:, LANES:] + _mm3(kh_t, v2)
        z = z_ref[0, p]
        z2 = _split2(z)
        ys = y0 + _mm3(_split2(q), z2)
        z_ref[0, p] = _mm3(_split2(a_mat), z2) + g_mat
        y = ys[:C] + ys[C:]
        mu = _seg(y, e_head) * (1.0 / R_HEAD)
        d = y - mu
        var = _seg(d * d, e_head) * (1.0 / R_HEAD)
        yn = d * lax.rsqrt(var + GN_EPS) * lnw_ref[:, sl] + lnb_ref[:, sl]
        bonus = _seg(r * k * rk_ref[:, sl], e_head) * v
        y_ref[0, :, sl] = ((yn + bonus) * g_ref[0, :, sl]).astype(y_ref.dtype)


def _wkv(r, lw, k, v, kp, b, g, rk, lnw, lnb, z0):
    bsz, t, rw = r.shape
    n_pairs = rw // LANES
    seq = lambda bb, c: (bb, c, 0)
    const = lambda bb, c: (0, 0)
    zmap = lambda bb, c: (bb, 0, 0, 0)
    return pl.pallas_call(
        _wkv_kernel,
        grid=(bsz, t // CHUNK),
        in_specs=[pl.BlockSpec((1, CHUNK, rw), seq)] * 7 + [pl.BlockSpec((1, rw), const)] * 3
        + [pl.BlockSpec((1, n_pairs, LANES, LANES), zmap)],
        out_specs=[pl.BlockSpec((1, CHUNK, rw), seq), pl.BlockSpec((1, n_pairs, LANES, LANES), zmap)],
        out_shape=[jax.ShapeDtypeStruct((bsz, t, rw), BF16), jax.ShapeDtypeStruct((bsz, n_pairs, LANES, LANES), F32)],
        compiler_params=_cparams("parallel", "arbitrary"),
        name="wkv",
    )(r, lw, k, v, kp, b, g, rk, lnw, lnb, z0)


def _state_to_blockdiag(s):
    bsz, h, n, _ = s.shape
    zt = jnp.swapaxes(s, -1, -2).reshape(bsz, h // 2, 2, n, n)
    zero = jnp.zeros_like(zt[:, :, 0])
    top = jnp.concatenate([zt[:, :, 0], zero], axis=-1)
    bot = jnp.concatenate([zero, zt[:, :, 1]], axis=-1)
    return jnp.concatenate([top, bot], axis=-2)


def _blockdiag_to_state(z):
    n = R_HEAD
    za = z[:, :, :n, :n]
    zb = z[:, :, n:, n:]
    s = jnp.stack([za, zb], axis=2)
    s = s.reshape(z.shape[0], -1, n, n)
    return jnp.swapaxes(s, -1, -2)


def _attn_kernel(scale_out, lam_ref, q_ref, k_ref, v_ref, sg_ref, o_ref, m_ref, l_ref, acc_ref):
    tq = q_ref.shape[1]
    tk = tq
    qi = pl.program_id(2)
    q = q_ref[0]
    first = lax.broadcasted_iota(jnp.int32, (tq, LANES), 1) < A_DK
    zero = jnp.zeros_like(q)
    q2 = jnp.concatenate([jnp.where(first, q, zero), jnp.where(first, zero, q)], axis=0)
    m_ref[...] = jnp.full(m_ref.shape, -jnp.inf, F32)
    l_ref[...] = jnp.zeros(l_ref.shape, F32)
    acc_ref[...] = jnp.zeros(acc_ref.shape, F32)
    reps = tk // LANES

    def update(s, vb):
        m_prev = m_ref[...]
        m_next = jnp.maximum(m_prev, jnp.max(s, axis=1, keepdims=True))
        p = jnp.exp(s - jnp.concatenate([m_next] * reps, axis=1))
        alpha = jnp.exp(m_prev - m_next)
        l_ref[...] = alpha * l_ref[...] + jnp.sum(p, axis=1, keepdims=True)
        acc_ref[...] = alpha * acc_ref[...] + jnp.dot(p.astype(BF16), vb, preferred_element_type=F32)
        m_ref[...] = m_next

    def body(j, carry):
        start = pl.multiple_of(j * tk, tk)
        kb = k_ref[0, pl.ds(start, tk), :]
        vb = v_ref[0, pl.ds(start, tk), :]
        update(lax.dot_general(q2, kb, NT, preferred_element_type=F32), vb)
        return carry

    lax.fori_loop(0, qi, body, 0)
    start = pl.multiple_of(qi * tk, tk)
    kb = k_ref[0, pl.ds(start, tk), :]
    vb = v_ref[0, pl.ds(start, tk), :]
    s = lax.dot_general(q2, kb, NT, preferred_element_type=F32)
    row = lax.broadcasted_iota(jnp.int32, (2 * tq, tk), 0)
    col = lax.broadcasted_iota(jnp.int32, (2 * tq, tk), 1)
    row = jnp.where(row >= tq, row - tq, row)
    update(jnp.where(col <= row, s, -jnp.inf), vb)
    o = acc_ref[...] / l_ref[...]
    att = o[:tq] - lam_ref[0] * o[tq:]
    att = att * lax.rsqrt(jnp.mean(att * att, axis=-1, keepdims=True) + NORM_EPS) * sg_ref[...]
    o_ref[0] = (att * scale_out).astype(o_ref.dtype)


def _attn_prompt(q, k, v, lam, subln, scale_out, tq):
    bsz, t, w = q.shape
    heads = w // LANES
    return pl.pallas_call(
        functools.partial(_attn_kernel, scale_out),
        grid=(bsz, heads, t // tq),
        in_specs=[
            pl.BlockSpec(memory_space=pltpu.SMEM),
            pl.BlockSpec((1, tq, LANES), lambda b, h, i: (b, i, h)),
            pl.BlockSpec((1, t, LANES), lambda b, h, i: (b, 0, h)),
            pl.BlockSpec((1, t, LANES), lambda b, h, i: (b, 0, h)),
            pl.BlockSpec((1, LANES), lambda b, h, i: (0, 0)),
        ],
        out_specs=pl.BlockSpec((1, tq, LANES), lambda b, h, i: (b, i, h)),
        out_shape=jax.ShapeDtypeStruct((bsz, t, w), BF16),
        scratch_shapes=[pltpu.VMEM((2 * tq, LANES), F32)] * 3,
        compiler_params=_cparams("parallel", "parallel", "arbitrary"),
        name="attn_prompt",
    )(lam, q, k, v, subln)


def _decode_kernel(scale_out, n_pages_step, pt_ref, lam_ref, q_ref, kn_ref, vn_ref, *refs):
    k_refs = refs[:n_pages_step]
    v_refs = refs[n_pages_step:2 * n_pages_step]
    sg_ref, o_ref, m_ref, l_ref, acc_ref = refs[2 * n_pages_step:]
    j = pl.program_id(1)
    rows = q_ref.shape[1]
    hrows = rows // A_HEADS
    tpad = hrows // 2

    @pl.when(j == 0)
    def _():
        m_ref[...] = jnp.full(m_ref.shape, -jnp.inf, F32)
        l_ref[...] = jnp.zeros(l_ref.shape, F32)
        acc_ref[...] = jnp.zeros(acc_ref.shape, F32)

    q = q_ref[0]

    def update(s_list, v_fn):
        m_prev = m_ref[...]
        m_cur = s_list[0].max(axis=1, keepdims=True)
        for s in s_list[1:]:
            m_cur = jnp.maximum(m_cur, s.max(axis=1, keepdims=True))
        m_next = jnp.maximum(m_prev, m_cur)
        alpha = jnp.exp(m_prev - m_next)
        l_new = alpha * l_ref[...]
        p_list = []
        for s in s_list:
            p = jnp.exp(s - m_next)
            l_new = l_new + jnp.sum(p, axis=1, keepdims=True)
            p_list.append(p.astype(BF16))
        l_ref[...] = l_new
        m_ref[...] = m_next
        for h in range(A_HEADS):
            hs = slice(h * hrows, (h + 1) * hrows)
            acc = alpha[hs] * acc_ref[hs, :]
            for i, p in enumerate(p_list):
                acc = acc + jnp.dot(p[hs], v_fn(i, h).astype(BF16), preferred_element_type=F32)
            acc_ref[hs, :] = acc

    s_list = [jnp.dot(q, kr[0].astype(BF16), preferred_element_type=F32) for kr in k_refs]
    update(s_list, lambda i, h: v_refs[i][0, pl.ds(h, PAGE_SIZE, stride=A_HEADS), :])

    @pl.when(j == pl.num_programs(1) - 1)
    def _():
        s = jnp.dot(q, kn_ref[0].astype(BF16), preferred_element_type=F32)
        tok = lax.broadcasted_iota(jnp.int32, s.shape, 0) & (tpad - 1)
        key = lax.broadcasted_iota(jnp.int32, s.shape, 1)
        update([jnp.where(key <= tok, s, -jnp.inf)], lambda i, h: vn_ref[0, :, h * LANES:(h + 1) * LANES])
        inv_l = 1.0 / l_ref[...]
        for h in range(A_HEADS):
            r0 = slice(h * hrows, h * hrows + tpad)
            r1 = slice(h * hrows + tpad, (h + 1) * hrows)
            att = acc_ref[r0, :] * inv_l[r0] - lam_ref[0] * (acc_ref[r1, :] * inv_l[r1])
            att = att * lax.rsqrt(jnp.mean(att * att, axis=-1, keepdims=True) + NORM_EPS) * sg_ref[...]
            o_ref[0, :, h * LANES:(h + 1) * LANES] = (att * scale_out).astype(o_ref.dtype)


def _attn_decode(qbig, kt_new, v_new, cache_kt, cache_v, page_ids, lam, subln, scale_out):
    bsz, rows, w = qbig.shape
    n_pages = page_ids.shape[1]
    pps = PAGES_PER_STEP
    tpad = rows // (2 * A_HEADS)

    def page_spec(i):
        return pl.BlockSpec((1, w, PAGE_SIZE), lambda b, j, pt: (pt[b, j * pps + i], 0, 0))

    per_b = lambda b, j, pt: (b, 0, 0)
    grid_spec = pltpu.PrefetchScalarGridSpec(
        num_scalar_prefetch=1,
        grid=(bsz, n_pages // pps),
        in_specs=[
            pl.BlockSpec(memory_space=pltpu.SMEM),
            pl.BlockSpec((1, rows, w), per_b),
            pl.BlockSpec((1, w, PAGE_SIZE), per_b),
            pl.BlockSpec((1, PAGE_SIZE, w), per_b),
        ] + [page_spec(i) for i in range(pps)] + [page_spec(i) for i in range(pps)]
        + [pl.BlockSpec((1, LANES), lambda b, j, pt: (0, 0))],
        out_specs=pl.BlockSpec((1, tpad, w), per_b),
        scratch_shapes=[pltpu.VMEM((rows, LANES), F32)] * 3,
    )
    return pl.pallas_call(
        functools.partial(_decode_kernel, scale_out, pps),
        grid_spec=grid_spec,
        out_shape=jax.ShapeDtypeStruct((bsz, tpad, w), BF16),
        compiler_params=_cparams("parallel", "arbitrary"),
        name="attn_decode",
    )(page_ids, lam, qbig, kt_new, v_new, *([cache_kt] * pps), *([cache_v] * pps), subln)


def _merge_kernel(rw_ref, att_ref, gate_ref, x_ref, gt_ref, brw_ref, bra_ref, wo_ref, o_ref):
    d = x_ref.shape[1]
    m = (gate_ref[:, :d] * jnp.dot(rw_ref[...], brw_ref[...], preferred_element_type=F32)
         + gate_ref[:, d:] * jnp.dot(att_ref[...], bra_ref[...], preferred_element_type=F32))
    o_ref[...] = x_ref[...] + gt_ref[0] * jnp.dot(m.astype(BF16), wo_ref[...], preferred_element_type=F32)


def _merge(rw, att, gates, x, gt, brw, bra, wo, tm, tps):
    rows, d = x.shape
    row = lambda i: (i, 0)
    const = lambda i: (0, 0)
    return pl.pallas_call(
        _merge_kernel,
        grid=(rows // tm,),
        in_specs=[
            pl.BlockSpec((tm, rw.shape[1]), row),
            pl.BlockSpec((tm, att.shape[1]), row),
            pl.BlockSpec((tm, gates.shape[1]), row),
            pl.BlockSpec((tm, d), row),
            pl.BlockSpec((1, gt.shape[1], d), lambda i: (i // tps, 0, 0)),
            pl.BlockSpec(brw.shape, const),
            pl.BlockSpec(bra.shape, const),
            pl.BlockSpec(wo.shape, const),
        ],
        out_specs=pl.BlockSpec((tm, d), row),
        out_shape=jax.ShapeDtypeStruct((rows, d), F32),
        compiler_params=_cparams("parallel"),
        name="merge_out",
    )(rw, att, gates, x, gt, brw, bra, wo)


def _mlp_kernel(fc, x_ref, sc_ref, sh_ref, gt_ref, g_ref, w1_ref, w2_ref, o_ref):
    x = x_ref[...]
    hb = _rms_mod(x, g_ref[...], sc_ref[0], sh_ref[0]).astype(BF16)
    acc = jnp.zeros(x.shape, F32)
    for c in range(w1_ref.shape[1] // fc):
        u = jnp.dot(hb, w1_ref[:, c * fc:(c + 1) * fc], preferred_element_type=F32)
        u = jnp.square(jnp.maximum(u, 0.0)).astype(BF16)
        acc = acc + jnp.dot(u, w2_ref[c * fc:(c + 1) * fc, :], preferred_element_type=F32)
    o_ref[...] = x + gt_ref[0] * acc


def _mlp(x, mods, g, w1, w2, tm, tps):
    rows, d = x.shape
    sc, sh, gt = mods
    r_mod = sc.shape[1]
    row = lambda i: (i, 0)
    const = lambda i: (0, 0)
    modspec = pl.BlockSpec((1, r_mod, d), lambda i: (i // tps, 0, 0))
    return pl.pallas_call(
        functools.partial(_mlp_kernel, 1024),
        grid=(rows // tm,),
        in_specs=[pl.BlockSpec((tm, d), row), modspec, modspec, modspec, pl.BlockSpec((1, d), const),
                  pl.BlockSpec(w1.shape, const, pipeline_mode=pl.Buffered(1)),
                  pl.BlockSpec(w2.shape, const, pipeline_mode=pl.Buffered(1))],
        out_specs=pl.BlockSpec((tm, d), row),
        out_shape=jax.ShapeDtypeStruct((rows, d), F32),
        compiler_params=_cparams("parallel"),
        name="mlp",
    )(x, sc, sh, gt, g, w1, w2)


def _rope_tables(pos):
    half = ROT_DIM // 2
    inv = ROPE_THETA ** (-jnp.arange(half, dtype=F32) / half)
    ang = pos.astype(F32)[:, None] * inv[None, :]
    cos, sin = jnp.cos(ang), jnp.sin(ang)
    t = pos.shape[0]
    ones = jnp.ones((t, A_DK - ROT_DIM), F32)
    zeros_h = jnp.zeros((t, half), F32)
    zeros_r = jnp.zeros((t, A_DK - ROT_DIM), F32)
    c = jnp.concatenate([cos, cos, ones], axis=1)
    s1 = jnp.concatenate([zeros_h, sin, zeros_r], axis=1)
    s2 = jnp.concatenate([-sin, zeros_h, zeros_r], axis=1)
    rep = lambda a: jnp.concatenate([a] * (LANES // A_DK), axis=1)
    return rep(c), rep(s1), rep(s2)


def _block_ones(n, blk):
    i = jnp.arange(n) // blk
    return (i[:, None] == i[None, :]).astype(BF16)


def _split_mod(mod, per_row_reps):
    parts = jnp.split(mod, 6, axis=-1)
    if per_row_reps is None:
        return [m[:, None, :] for m in parts]
    return [jnp.repeat(m, per_row_reps, axis=0)[None] for m in parts]


def kernel(x_prompt, x_sample, c_prompt, c_sample, cache_k, cache_v, state_wkv, state_shift, page_table, norm1_g,
           norm2_g, ada_w, ada_b, w_in, shift_mu, rw_w0, rw_wb, rw_a0, rw_ab, rw_gb, rw_v0, rw_va, rw_vb, rw_kk,
           rw_ka, rw_rk, rw_ln_w, rw_ln_b, at_qnorm, at_knorm, at_lq1, at_lk1, at_lq2, at_lk2, at_subln, br_rwkv,
           br_attn, w_out, mlp_w1, mlp_w2):
    bp, tp, d = x_prompt.shape
    bs, ts, _ = x_sample.shape
    depth = w_in.shape[0]
    rw = rw_w0.shape[1]
    shift_cols = shift_mu.shape[1]
    qk_w = A_HEADS * 2 * A_DK
    v_w = A_HEADS * 2 * A_DK
    n_phys = cache_k.shape[1]
    n_pages = page_table.shape[1]
    past_len = n_pages * PAGE_SIZE
    r_heads = rw // R_HEAD

    mod_all = _ada_mod(jnp.concatenate([c_prompt, c_sample], axis=0), ada_w, ada_b)
    rope_p = _rope_tables(jnp.arange(tp))
    rope_s = tuple(jnp.tile(a, (bs, 1)) for a in _rope_tables(past_len + jnp.arange(ts)))
    e_qk = _block_ones(qk_w, A_DK)
    e_rw = _block_ones(rw, R_HEAD)
    ckt = jnp.transpose(cache_k, (0, 1, 3, 4, 5, 2)).reshape(depth * n_phys, qk_w, PAGE_SIZE)
    cv = cache_v.reshape(depth * n_phys, PAGE_SIZE * A_HEADS, v_w // A_HEADS)
    tpad = SUBLANES
    lane = jnp.arange(qk_w)
    q_own = ((lane // LANES)[None, None, :] == jnp.arange(A_HEADS)[:, None, None]) & (
        ((lane // A_DK) % 2)[None, None, :] == jnp.arange(2)[None, :, None])

    xp = x_prompt.reshape(bp * tp, d)
    xs = x_sample.reshape(bs * ts, d)
    rows_s = bs * ts
    tm_p = 256
    outs = {n: [] for n in ("kp", "vp", "ks", "vs", "sp", "ss", "shp", "shs")}
    vfp = vfs = None
    for l in range(depth):
        lam_init = 0.8 - 0.6 * math.exp(-0.3 * l)
        lam = (jnp.exp(jnp.sum(at_lq1[l] * at_lk1[l]).astype(F32)) - jnp.exp(jnp.sum(at_lq2[l] * at_lk2[l]).astype(F32))
               + lam_init).reshape(1)
        pad_w = jnp.zeros((A_RANK, rw), F32)
        p = dict(
            shift_mu=shift_mu[l][None], rw_w0=rw_w0[l][None], rw_a0=rw_a0[l][None],
            wb_pad=jnp.concatenate([rw_wb[l], pad_w], axis=0).astype(BF16),
            ab_pad=jnp.concatenate([jnp.zeros((W_RANK, rw), F32), rw_ab[l]], axis=0).astype(BF16),
            rw_gb=rw_gb[l].astype(BF16), rw_kk=rw_kk[l][None], rw_ka=rw_ka[l][None])
        if l > 0:
            p.update(rw_v0=rw_v0[l - 1][None], rw_va=rw_va[l - 1].astype(BF16), rw_vb=rw_vb[l - 1].astype(BF16))
        w_in_b = w_in[l].astype(BF16)
        qn = jnp.tile(at_qnorm[l], qk_w // A_DK)[None]
        kn = jnp.tile(at_knorm[l], qk_w // A_DK)[None]
        subln = at_subln[l][None]
        rk = rw_rk[l].reshape(1, rw)
        lnw, lnb = rw_ln_w[l][None], rw_ln_b[l][None]
        brw, bra, wo = br_rwkv[l].astype(BF16), br_attn[l].astype(BF16), w_out[l].astype(BF16)
        w1, w2 = mlp_w1[l].astype(BF16), mlp_w2[l].astype(BF16)
        g1, g2 = norm1_g[l][None], norm2_g[l][None]
        scale_out = 1.0 - lam_init

        sh1, sc1, gt1, sh2, sc2, gt2 = _split_mod(mod_all[l, :bp], None)
        tps = tp // tm_p
        zs, q, k, kb, v, vb, gates = _inproj(xp, (sc1, sh1), g1, w_in_b, qn, kn, e_qk, rope_p, shift_cols, qk_w, v_w,
                                             tm_p, tps)
        att = _attn_prompt(q.reshape(bp, tp, qk_w), kb.reshape(bp, tp, qk_w), vb.reshape(bp, tp, v_w), lam, subln,
                           scale_out, 256).reshape(bp * tp, v_w)
        first = jnp.zeros((bp, 1, shift_cols), F32)
        r_, lw_, k_, v_, kp_, b_, g_ = _rwkv_prep(zs, zs, first, p, e_rw, vfp, True, tm_p, tps)
        if l == 0:
            vfp = v_
        sh3 = lambda a: a.reshape(bp, tp, rw)
        z0 = jnp.zeros((bp, r_heads // 2, LANES, LANES), F32)
        rwo, zfin = _wkv(sh3(r_), sh3(lw_), sh3(k_), sh3(v_), sh3(kp_), sh3(b_), sh3(g_), rk, lnw, lnb, z0)
        x1 = _merge(rwo.reshape(bp * tp, rw), att, gates, xp, gt1, brw, bra, wo, 512, tp // 512)
        xp = _mlp(x1, (sc2, sh2, gt2), g2, w1, w2, 512, tp // 512)
        outs["kp"].append(k.reshape(bp, tp, A_HEADS, 2, A_DK))
        outs["vp"].append(v.reshape(bp, tp, A_HEADS, 2 * A_DK))
        outs["sp"].append(_blockdiag_to_state(zfin))
        outs["shp"].append(zs.reshape(bp, tp, shift_cols)[:, -1])

        sh1, sc1, gt1, sh2, sc2, gt2 = _split_mod(mod_all[l, bp:], ts)
        zs, q, k, kb, v, vb, gates = _inproj(xs, (sc1, sh1), g1, w_in_b, qn, kn, e_qk, rope_s, shift_cols, qk_w, v_w,
                                             rows_s, 1)
        q3 = jnp.pad(q.reshape(bs, ts, qk_w), ((0, 0), (0, tpad - ts), (0, 0)))
        qbig = jnp.where(q_own[None, :, :, None, :], q3[:, None, None], 0).reshape(bs, 2 * A_HEADS * tpad, qk_w)
        pad_new = lambda a: jnp.pad(a.reshape(bs, ts, -1), ((0, 0), (0, PAGE_SIZE - ts), (0, 0)))
        att = _attn_decode(qbig, jnp.swapaxes(pad_new(k), 1, 2), pad_new(v), ckt, cv, page_table + l * n_phys, lam,
                           subln, scale_out)
        att = att[:, :ts].reshape(rows_s, v_w)
        zs3 = zs.reshape(bs, ts, shift_cols)
        prev = jnp.concatenate([state_shift[l][:, None, :], zs3[:, :-1]], axis=1).reshape(rows_s, shift_cols)
        r_, lw_, k_, v_, kp_, b_, g_ = _rwkv_prep(zs, prev, jnp.zeros((1, 1, shift_cols), F32), p, e_rw, vfs, False,
                                                  rows_s, 1)
        if l == 0:
            vfs = v_
        padc = lambda a: jnp.pad(a.reshape(bs, ts, rw), ((0, 0), (0, CHUNK - ts), (0, 0)))
        rwo, zfin = _wkv(padc(r_), padc(lw_), padc(k_), padc(v_), padc(kp_), padc(b_), padc(g_), rk, lnw, lnb,
                         _state_to_blockdiag(state_wkv[l]))
        rwo = rwo[:, :ts].reshape(rows_s, rw)
        x1 = _merge(rwo, att, gates, xs, gt1, brw, bra, wo, rows_s, 1)
        xs = _mlp(x1, (sc2, sh2, gt2), g2, w1, w2, rows_s, 1)
        outs["ks"].append(k.reshape(bs, ts, A_HEADS, 2, A_DK))
        outs["vs"].append(v.reshape(bs, ts, A_HEADS, 2 * A_DK))
        outs["ss"].append(_blockdiag_to_state(zfin))
        outs["shs"].append(zs3[:, -1])

    st = lambda n: jnp.stack(outs[n])
    return (xp.reshape(bp, tp, d), xs.reshape(bs, ts, d), st("kp"), st("vp"), st("ks"), st("vs"), st("sp"), st("ss"),
            st("shp"), st("shs"))
```

```python
import functools
import math

import jax
import jax.numpy as jnp
from jax import lax
from jax.experimental import pallas as pl
from jax.experimental.pallas import tpu as pltpu

F32, BF16 = jnp.float32, jnp.bfloat16

R_HEAD = 64
GN_EPS = 64e-5
A_HEADS = 4
A_DK = 64
ROT_DIM = A_DK // 4
ROPE_THETA = 500000.0
NORM_EPS = 1e-6
PAGE_SIZE = 128
W_RANK, A_RANK, G_RANK = 64, 64, 128

LANES = 128
SUBLANES = 8
VMEM_LIMIT = 56 * 1024 * 1024

CHUNK = 64
PAGES_PER_STEP = 16

NN = (((1,), (0,)), ((), ()))
NT = (((1,), (1,)), ((), ()))


def _cparams(*sem):
    return pltpu.CompilerParams(dimension_semantics=sem, vmem_limit_bytes=VMEM_LIMIT)


def _dot(a, b, dims=NN):
    return lax.dot_general(a.astype(BF16), b.astype(BF16), dims, preferred_element_type=F32)


def _split2(x):
    hi = x.astype(BF16)
    lo = (x - hi.astype(F32)).astype(BF16)
    return hi, lo


def _mm3(a, b, dims=NN):
    a_hi, a_lo = _split2(a)
    b_hi, b_lo = _split2(b)
    return lax.dot_general(jnp.concatenate([a_hi, a_lo, a_hi], axis=1),
                           jnp.concatenate([b_hi, b_hi, b_lo], axis=dims[0][1][0]), dims,
                           preferred_element_type=F32)


def _seg(x, e2):
    return jnp.dot(jnp.concatenate(_split2(x), axis=1), e2, preferred_element_type=F32)


def _sigmoid(x):
    return 1.0 / (1.0 + jnp.exp(-x))


def _rms_mod(x, g, sc, sh):
    y = x * lax.rsqrt(jnp.mean(x * x, axis=-1, keepdims=True) + NORM_EPS) * g
    return y * (1.0 + sc) + sh


def _ada_kernel(c_ref, w_ref, b_ref, o_ref):
    c = c_ref[...]
    o_ref[0] = _dot(c * _sigmoid(c), w_ref[0]) + b_ref[0]


def _ada_mod(c_all, ada_w, ada_b):
    depth, d, n = ada_w.shape
    rows = c_all.shape[0]
    tn = d
    return pl.pallas_call(
        _ada_kernel,
        grid=(depth, n // tn),
        in_specs=[
            pl.BlockSpec((rows, d), lambda l, j: (0, 0)),
            pl.BlockSpec((1, d, tn), lambda l, j: (l, 0, j)),
            pl.BlockSpec((1, 1, tn), lambda l, j: (l, 0, j)),
        ],
        out_specs=pl.BlockSpec((1, rows, tn), lambda l, j: (l, 0, j)),
        out_shape=jax.ShapeDtypeStruct((depth, rows, n), F32),
        compiler_params=_cparams("parallel", "parallel"),
        name="ada_mod",
    )(c_all, ada_w, ada_b.reshape(depth, 1, n))


def _inproj_kernel(shift_cols, qk_w, v_w, x_ref, sc_ref, sh_ref, g_ref, w_ref, qn_ref, kn_ref, e_ref,
                   cos_ref, s1_ref, s2_ref, zs_ref, q_ref, k_ref, kb_ref, v_ref, vb_ref, gate_ref):
    hb = _rms_mod(x_ref[...], g_ref[...], sc_ref[0], sh_ref[0]).astype(BF16)
    o = 0
    zs_ref[...] = jnp.dot(hb, w_ref[:, o:o + shift_cols], preferred_element_type=F32)
    o += shift_cols
    reps = qk_w // LANES
    cos = jnp.concatenate([cos_ref[...]] * reps, axis=1)
    s1 = jnp.concatenate([s1_ref[...]] * reps, axis=1)
    s2 = jnp.concatenate([s2_ref[...]] * reps, axis=1)

    def norm_rope(z, gain):
        ms = _seg(z * z, e_ref[...]) * (1.0 / A_DK)
        y = z * lax.rsqrt(ms + NORM_EPS) * gain
        return y * cos + pltpu.roll(y, ROT_DIM // 2, 1) * s1 + pltpu.roll(y, qk_w - ROT_DIM // 2, 1) * s2

    q = norm_rope(jnp.dot(hb, w_ref[:, o:o + qk_w], preferred_element_type=F32), qn_ref[...])
    q_ref[...] = (q * (A_DK ** -0.5)).astype(BF16)
    o += qk_w
    k = norm_rope(jnp.dot(hb, w_ref[:, o:o + qk_w], preferred_element_type=F32), kn_ref[...])
    k_ref[...] = k
    kb_ref[...] = k.astype(BF16)
    o += qk_w
    v = jnp.dot(hb, w_ref[:, o:o + v_w], preferred_element_type=F32)
    v_ref[...] = v
    vb_ref[...] = v.astype(BF16)
    o += v_w
    gate_ref[...] = _sigmoid(jnp.dot(hb, w_ref[:, o:], preferred_element_type=F32))


def _inproj(x, mods, g, w_bf, qn, kn, e_qk, rope, shift_cols, qk_w, v_w, tm, tps):
    rows, d = x.shape
    in_cols = w_bf.shape[1]
    gate_w = in_cols - shift_cols - 2 * qk_w - v_w
    sc, sh = mods
    r_mod = sc.shape[1]
    row = lambda i: (i, 0)
    const = lambda i: (0, 0)
    modmap = lambda i: (i // tps, 0, 0)
    ropemap = lambda i: (i % tps, 0)
    outs = [(shift_cols, F32), (qk_w, BF16), (qk_w, F32), (qk_w, BF16), (v_w, F32), (v_w, BF16), (gate_w, F32)]
    return pl.pallas_call(
        functools.partial(_inproj_kernel, shift_cols, qk_w, v_w),
        grid=(rows // tm,),
        in_specs=[
            pl.BlockSpec((tm, d), row),
            pl.BlockSpec((1, r_mod, d), modmap),
            pl.BlockSpec((1, r_mod, d), modmap),
            pl.BlockSpec((1, d), const),
            pl.BlockSpec((d, in_cols), const),
            pl.BlockSpec((1, qk_w), const),
            pl.BlockSpec((1, qk_w), const),
            pl.BlockSpec((2 * qk_w, qk_w), const),
            pl.BlockSpec((tm, LANES), ropemap),
            pl.BlockSpec((tm, LANES), ropemap),
            pl.BlockSpec((tm, LANES), ropemap),
        ],
        out_specs=[pl.BlockSpec((tm, w), row) for w, _ in outs],
        out_shape=[jax.ShapeDtypeStruct((rows, w), dt) for w, dt in outs],
        compiler_params=_cparams("parallel"),
        name="inproj",
    )(x, sc, sh, g, w_bf, qn, kn, e_qk, *rope)


def _prep_kernel(halo, has_vres, tps, rw, *refs):
    if has_vres:
        (zs_ref, prev_ref, first_ref, mu_ref, w0_ref, wb_ref, a0_ref, ab_ref, gb_ref, kk_ref, ka_ref, e_ref,
         v0_ref, va_ref, vb_ref, vf_ref, r_o, lw_o, k_o, v_o, kp_o, b_o, g_o) = refs
    else:
        (zs_ref, prev_ref, first_ref, mu_ref, w0_ref, wb_ref, a0_ref, ab_ref, gb_ref, kk_ref, ka_ref, e_ref,
         r_o, lw_o, k_o, v_o, kp_o, b_o, g_o) = refs
    z = zs_ref[...]
    if halo:
        tm = z.shape[0]
        rolled = pltpu.roll(z, 1, 0)
        first = jnp.where(pl.program_id(0) % tps == 0, first_ref[0], prev_ref[SUBLANES - 1:SUBLANES, :])
        prev = jnp.where(lax.broadcasted_iota(jnp.int32, (tm, 1), 0) == 0, first, rolled)
    else:
        prev = prev_ref[...]
    xs = z + (prev - z) * mu_ref[...]
    r = xs[:, 0:rw]
    k = xs[:, rw:2 * rw]
    v = xs[:, 2 * rw:3 * rw]
    zwa = xs[:, 3 * rw:3 * rw + W_RANK + A_RANK]
    zg = xs[:, 3 * rw + W_RANK + A_RANK:]
    u = -(w0_ref[...] + _dot(jnp.tanh(zwa), wb_ref[...]))
    softplus = jnp.maximum(u, 0.0) + jnp.log(1.0 + jnp.exp(-jnp.abs(u)))
    lw_o[...] = -jnp.exp(-softplus - 0.5)
    a = _sigmoid(a0_ref[...] + _dot(zwa, ab_ref[...]))
    g_o[...] = _dot(_sigmoid(zg), gb_ref[...])
    if has_vres:
        vg = _sigmoid(v0_ref[...] + _dot(_dot(v, va_ref[...]), vb_ref[...]))
        v = v + (vf_ref[...] - v) * vg
    kk = k * kk_ref[...]
    nrm = jnp.sqrt(_seg(kk * kk, e_ref[...]))
    kk = kk / jnp.maximum(nrm, 1e-12)
    r_o[...] = r
    k_o[...] = k * (1.0 + (a - 1.0) * ka_ref[...])
    v_o[...] = v
    kp_o[...] = kk
    b_o[...] = kk * a


def _rwkv_prep(zs, prev_src, first, p, e_rw, v_first, halo, tm, tps):
    rows, cols = zs.shape
    rw = p["rw_w0"].shape[1]
    has_vres = v_first is not None
    row = lambda i: (i, 0)
    const = lambda i: (0, 0)
    full = lambda a: pl.BlockSpec(a.shape, const)
    if halo:
        prev_spec = pl.BlockSpec((SUBLANES, cols), lambda i: (jnp.maximum(i * (tm // SUBLANES) - 1, 0), 0))
        first_spec = pl.BlockSpec((1, 1, cols), lambda i: (i // tps, 0, 0))
    else:
        prev_spec = pl.BlockSpec((tm, cols), row)
        first_spec = pl.BlockSpec((1, 1, cols), lambda i: (0, 0, 0))
    params = [p["shift_mu"], p["rw_w0"], p["wb_pad"], p["rw_a0"], p["ab_pad"], p["rw_gb"], p["rw_kk"], p["rw_ka"],
              e_rw]
    args = [zs, prev_src, first] + params
    in_specs = [pl.BlockSpec((tm, cols), row), prev_spec, first_spec] + [full(a) for a in params]
    if has_vres:
        extra = [p["rw_v0"], p["rw_va"], p["rw_vb"]]
        args += extra + [v_first]
        in_specs += [full(a) for a in extra] + [pl.BlockSpec((tm, rw), row)]
    return pl.pallas_call(
        functools.partial(_prep_kernel, halo, has_vres, tps, rw),
        grid=(rows // tm,),
        in_specs=in_specs,
        out_specs=[pl.BlockSpec((tm, rw), row)] * 7,
        out_shape=[jax.ShapeDtypeStruct((rows, rw), F32)] * 7,
        compiler_params=_cparams("parallel"),
        name="rwkv_prep",
    )(*args)


def _wkv_kernel(r_ref, lw_ref, k_ref, v_ref, kp_ref, b_ref, g_ref, rk_ref, lnw_ref, lnb_ref, z0_ref, y_ref, z_ref):
    C = CHUNK
    P2 = 2 * C

    @pl.when(pl.program_id(1) == 0)
    def _():
        z_ref[...] = z0_ref[...]

    ri = lax.broadcasted_iota(jnp.int32, (P2, P2), 0)
    ci = lax.broadcasted_iota(jnp.int32, (P2, P2), 1)
    strict = ci < ri
    incl = ci <= ri
    eye = (ci == ri).astype(F32)
    blk = lambda s: (ri >> s) == (ci >> s)
    e_pair = blk(6).astype(BF16)
    e_head = jnp.concatenate([e_pair, e_pair], axis=0)
    ti = lax.broadcasted_iota(jnp.int32, (C, C), 0)
    tj = lax.broadcasted_iota(jnp.int32, (C, C), 1)
    tril1 = (tj <= ti).astype(BF16)
    head_a = lax.broadcasted_iota(jnp.int32, (C, LANES), 1) < R_HEAD

    def stack(x):
        return jnp.concatenate([jnp.where(head_a, x, 0.0), jnp.where(head_a, 0.0, x)], axis=0)

    pairs = range(r_ref.shape[2] // LANES)
    sls = [slice(p * LANES, (p + 1) * LANES) for p in pairs]
    mm = _mm3
    cat = jnp.concatenate
    ld = lambda ref: [ref[0, :, sl] for sl in sls]
    r, lw, k, v, kp, b = ld(r_ref), ld(lw_ref), ld(k_ref), ld(v_ref), ld(kp_ref), ld(b_ref)

    def cumsum(x):
        x0 = x.astype(BF16)
        x1 = x - x0.astype(F32)
        x1b = x1.astype(BF16)
        x2b = (x1 - x1b.astype(F32)).astype(BF16)
        return jnp.dot(cat([tril1] * 3, axis=1), cat([x0, x1b, x2b], axis=0), preferred_element_type=F32)

    logp = [cumsum(x) for x in lw]
    logpc = [x[C - 1:C, :] for x in logp]
    e_neg = [jnp.exp(-x) for x in logp]
    e_end = [jnp.exp(xc - x) for xc, x in zip(logpc, logp)]
    rt = [stack(r[p] * jnp.exp(logp[p])) for p in pairs]
    kpt = [stack(kp[p] * jnp.exp(logp[p] - lw[p])) for p in pairs]
    v_s = [stack(v[p]) for p in pairs]
    gram = [_mm3(cat([kpt[p], rt[p]], axis=0), cat([stack(b[p] * e_neg[p]), stack(k[p] * e_neg[p])], axis=0), NT)
            for p in pairs]
    lb = [jnp.where(strict, g[:P2, :P2], 0.0) for g in gram]
    lk = [jnp.where(strict, g[:P2, P2:], 0.0) for g in gram]
    mb = [jnp.where(incl, g[P2:, :P2], 0.0) for g in gram]
    mk = [jnp.where(incl, g[P2:, P2:], 0.0) for g in gram]
    x = [mm(lk[p], v_s[p]) for p in pairs]
    n1 = [jnp.where(blk(3), -m, 0.0) for m in lb]
    n2 = [mm(m, m) for m in n1]
    t = [mm(eye + a, eye + c) for a, c in zip(n1, n2)]
    n4 = [mm(m, m) for m in n2]
    t = [mm(a, eye + c) for a, c in zip(t, n4)]
    for s in (3, 4, 5):
        off_mask = blk(s + 1) & jnp.logical_not(blk(s))
        w = [mm(jnp.where(off_mask, lb[p], 0.0), t[p]) for p in pairs]
        t = [t[p] - mm(t[p], w[p]) for p in pairs]
    wu = [-mm(t[p], cat([kpt[p], x[p]], axis=1)) for p in pairs]
    lhs = [cat([cat([mb[p], mk[p]], axis=1),
                cat([stack(b[p] * e_end[p]).T, stack(k[p] * e_end[p]).T], axis=1)], axis=0) for p in pairs]
    rhs = [cat([wu[p], cat([jnp.zeros_like(v_s[p]), v_s[p]], axis=1)], axis=0) for p in pairs]
    big = [mm(lhs[p], rhs[p]) for p in pairs]
    qa = [cat([rt[p] + big[p][:P2, :LANES], eye * jnp.exp(logpc[p]) + big[p][P2:, :LANES]], axis=0) for p in pairs]
    qaz = [mm(qa[p], z_ref[0, p]) for p in pairs]
    for p in pairs:
        z_ref[0, p] = qaz[p][P2:] + big[p][P2:, LANES:]
    for p in pairs:
        ys = qaz[p][:P2] + big[p][:P2, LANES:]
        y = ys[:C] + ys[C:]
        mu = _seg(y, e_head) * (1.0 / R_HEAD)
        d = y - mu
        var = _seg(d * d, e_head) * (1.0 / R_HEAD)
        yn = d * lax.rsqrt(var + GN_EPS) * lnw_ref[:, sls[p]] + lnb_ref[:, sls[p]]
        bonus = _seg(r[p] * k[p] * rk_ref[:, sls[p]], e_head) * v[p]
        y_ref[0, :, sls[p]] = ((yn + bonus) * g_ref[0, :, sls[p]]).astype(y_ref.dtype)


def _wkv(r, lw, k, v, kp, b, g, rk, lnw, lnb, z0):
    bsz, t, rw = r.shape
    n_pairs = rw // LANES
    seq = lambda bb, c: (bb, c, 0)
    const = lambda bb, c: (0, 0)
    zmap = lambda bb, c: (bb, 0, 0, 0)
    return pl.pallas_call(
        _wkv_kernel,
        grid=(bsz, t // CHUNK),
        in_specs=[pl.BlockSpec((1, CHUNK, rw), seq)] * 7 + [pl.BlockSpec((1, rw), const)] * 3
        + [pl.BlockSpec((1, n_pairs, LANES, LANES), zmap)],
        out_specs=[pl.BlockSpec((1, CHUNK, rw), seq), pl.BlockSpec((1, n_pairs, LANES, LANES), zmap)],
        out_shape=[jax.ShapeDtypeStruct((bsz, t, rw), BF16), jax.ShapeDtypeStruct((bsz, n_pairs, LANES, LANES), F32)],
        compiler_params=_cparams("parallel", "arbitrary"),
        name="wkv",
    )(r, lw, k, v, kp, b, g, rk, lnw, lnb, z0)


def _state_to_blockdiag(s):
    bsz, h, n, _ = s.shape
    zt = jnp.swapaxes(s, -1, -2).reshape(bsz, h // 2, 2, n, n)
    zero = jnp.zeros_like(zt[:, :, 0])
    top = jnp.concatenate([zt[:, :, 0], zero], axis=-1)
    bot = jnp.concatenate([zero, zt[:, :, 1]], axis=-1)
    return jnp.concatenate([top, bot], axis=-2)


def _blockdiag_to_state(z):
    n = R_HEAD
    za = z[:, :, :n, :n]
    zb = z[:, :, n:, n:]
    s = jnp.stack([za, zb], axis=2)
    s = s.reshape(z.shape[0], -1, n, n)
    return jnp.swapaxes(s, -1, -2)


def _attn_kernel(scale_out, tk, lam_ref, q_ref, k_ref, v_ref, sg_ref, o_ref, m_ref, l_ref, acc_ref):
    tq = q_ref.shape[1]
    qi = pl.program_id(2)
    jd = qi // (tk // tq)
    q = q_ref[0]
    first = lax.broadcasted_iota(jnp.int32, (tq, LANES), 1) < A_DK
    zero = jnp.zeros_like(q)
    q2 = jnp.concatenate([jnp.where(first, q, zero), jnp.where(first, zero, q)], axis=0)
    m_ref[...] = jnp.full(m_ref.shape, -jnp.inf, F32)
    l_ref[...] = jnp.zeros(l_ref.shape, F32)
    acc_ref[...] = jnp.zeros(acc_ref.shape, F32)
    reps = tk // LANES

    def update(s, vb):
        m_prev = m_ref[...]
        m_next = jnp.maximum(m_prev, jnp.max(s, axis=1, keepdims=True))
        p = jnp.exp(s - jnp.concatenate([m_next] * reps, axis=1))
        alpha = jnp.exp(m_prev - m_next)
        l_ref[...] = alpha * l_ref[...] + jnp.sum(p, axis=1, keepdims=True)
        acc_ref[...] = alpha * acc_ref[...] + jnp.dot(p.astype(BF16), vb, preferred_element_type=F32)
        m_ref[...] = m_next

    def body(j, carry):
        start = pl.multiple_of(j * tk, tk)
        kb = k_ref[0, pl.ds(start, tk), :]
        vb = v_ref[0, pl.ds(start, tk), :]
        update(lax.dot_general(q2, kb, NT, preferred_element_type=F32), vb)
        return carry

    lax.fori_loop(0, jd, body, 0)
    start = pl.multiple_of(jd * tk, tk)
    kb = k_ref[0, pl.ds(start, tk), :]
    vb = v_ref[0, pl.ds(start, tk), :]
    s = lax.dot_general(q2, kb, NT, preferred_element_type=F32)
    row = lax.broadcasted_iota(jnp.int32, (2 * tq, tk), 0)
    col = lax.broadcasted_iota(jnp.int32, (2 * tq, tk), 1)
    row = jnp.where(row >= tq, row - tq, row) + (qi * tq - jd * tk)
    update(jnp.where(col <= row, s, -jnp.inf), vb)
    o = acc_ref[...] / l_ref[...]
    att = o[:tq] - lam_ref[0] * o[tq:]
    att = att * lax.rsqrt(jnp.mean(att * att, axis=-1, keepdims=True) + NORM_EPS) * sg_ref[...]
    o_ref[0] = (att * scale_out).astype(o_ref.dtype)


def _attn_prompt(q, k, v, lam, subln, scale_out, tq, tk):
    bsz, t, w = q.shape
    heads = w // LANES
    return pl.pallas_call(
        functools.partial(_attn_kernel, scale_out, tk),
        grid=(bsz, heads, t // tq),
        in_specs=[
            pl.BlockSpec(memory_space=pltpu.SMEM),
            pl.BlockSpec((1, tq, LANES), lambda b, h, i: (b, i, h)),
            pl.BlockSpec((1, t, LANES), lambda b, h, i: (b, 0, h)),
            pl.BlockSpec((1, t, LANES), lambda b, h, i: (b, 0, h)),
            pl.BlockSpec((1, LANES), lambda b, h, i: (0, 0)),
        ],
        out_specs=pl.BlockSpec((1, tq, LANES), lambda b, h, i: (b, i, h)),
        out_shape=jax.ShapeDtypeStruct((bsz, t, w), BF16),
        scratch_shapes=[pltpu.VMEM((2 * tq, LANES), F32)] * 3,
        compiler_params=_cparams("parallel", "parallel", "arbitrary"),
        name="attn_prompt",
    )(lam, q, k, v, subln)


def _decode_kernel(scale_out, n_pages_step, pt_ref, lam_ref, q_ref, kn_ref, vn_ref, *refs):
    k_refs = refs[:n_pages_step]
    v_refs = refs[n_pages_step:2 * n_pages_step]
    sg_ref, o_ref, m_ref, l_ref, acc_ref = refs[2 * n_pages_step:]
    j = pl.program_id(1)
    rows = q_ref.shape[1]
    hrows = rows // A_HEADS
    tpad = hrows // 2

    @pl.when(j == 0)
    def _():
        m_ref[...] = jnp.full(m_ref.shape, -jnp.inf, F32)
        l_ref[...] = jnp.zeros(l_ref.shape, F32)
        acc_ref[...] = jnp.zeros(acc_ref.shape, F32)

    q = q_ref[0]

    def update(s_list, v_fn):
        m_prev = m_ref[...]
        m_cur = s_list[0].max(axis=1, keepdims=True)
        for s in s_list[1:]:
            m_cur = jnp.maximum(m_cur, s.max(axis=1, keepdims=True))
        m_next = jnp.maximum(m_prev, m_cur)
        alpha = jnp.exp(m_prev - m_next)
        l_new = alpha * l_ref[...]
        p_list = []
        for s in s_list:
            p = jnp.exp(s - m_next)
            l_new = l_new + jnp.sum(p, axis=1, keepdims=True)
            p_list.append(p.astype(BF16))
        l_ref[...] = l_new
        m_ref[...] = m_next
        for h in range(A_HEADS):
            hs = slice(h * hrows, (h + 1) * hrows)
            acc = alpha[hs] * acc_ref[hs, :]
            for i, p in enumerate(p_list):
                acc = acc + jnp.dot(p[hs], v_fn(i, h).astype(BF16), preferred_element_type=F32)
            acc_ref[hs, :] = acc

    s_list = [jnp.dot(q, kr[0].astype(BF16), preferred_element_type=F32) for kr in k_refs]
    update(s_list, lambda i, h: v_refs[i][0, pl.ds(h, PAGE_SIZE, stride=A_HEADS), :])

    @pl.when(j == pl.num_programs(1) - 1)
    def _():
        s = jnp.dot(q, kn_ref[0].astype(BF16), preferred_element_type=F32)
        tok = lax.broadcasted_iota(jnp.int32, s.shape, 0) & (tpad - 1)
        key = lax.broadcasted_iota(jnp.int32, s.shape, 1)
        update([jnp.where(key <= tok, s, -jnp.inf)], lambda i, h: vn_ref[0, :, h * LANES:(h + 1) * LANES])
        inv_l = 1.0 / l_ref[...]
        for h in range(A_HEADS):
            r0 = slice(h * hrows, h * hrows + tpad)
            r1 = slice(h * hrows + tpad, (h + 1) * hrows)
            att = acc_ref[r0, :] * inv_l[r0] - lam_ref[0] * (acc_ref[r1, :] * inv_l[r1])
            att = att * lax.rsqrt(jnp.mean(att * att, axis=-1, keepdims=True) + NORM_EPS) * sg_ref[...]
            o_ref[0, :, h * LANES:(h + 1) * LANES] = (att * scale_out).astype(o_ref.dtype)


def _attn_decode(qbig, kt_new, v_new, cache_kt, cache_v, page_ids, lam, subln, scale_out):
    bsz, rows, w = qbig.shape
    n_pages = page_ids.shape[1]
    pps = PAGES_PER_STEP
    tpad = rows // (2 * A_HEADS)

    def page_spec(i):
        return pl.BlockSpec((1, w, PAGE_SIZE), lambda b, j, pt: (pt[b, j * pps + i], 0, 0))

    per_b = lambda b, j, pt: (b, 0, 0)
    grid_spec = pltpu.PrefetchScalarGridSpec(
        num_scalar_prefetch=1,
        grid=(bsz, n_pages // pps),
        in_specs=[
            pl.BlockSpec(memory_space=pltpu.SMEM),
            pl.BlockSpec((1, rows, w), per_b),
            pl.BlockSpec((1, w, PAGE_SIZE), per_b),
            pl.BlockSpec((1, PAGE_SIZE, w), per_b),
        ] + [page_spec(i) for i in range(pps)] + [page_spec(i) for i in range(pps)]
        + [pl.BlockSpec((1, LANES), lambda b, j, pt: (0, 0))],
        out_specs=pl.BlockSpec((1, tpad, w), per_b),
        scratch_shapes=[pltpu.VMEM((rows, LANES), F32)] * 3,
    )
    return pl.pallas_call(
        functools.partial(_decode_kernel, scale_out, pps),
        grid_spec=grid_spec,
        out_shape=jax.ShapeDtypeStruct((bsz, tpad, w), BF16),
        compiler_params=_cparams("parallel", "arbitrary"),
        name="attn_decode",
    )(page_ids, lam, qbig, kt_new, v_new, *([cache_kt] * pps), *([cache_v] * pps), subln)


def _merge_kernel(rw_ref, att_ref, gate_ref, x_ref, gt_ref, brw_ref, bra_ref, wo_ref, o_ref):
    d = x_ref.shape[1]
    m = (gate_ref[:, :d] * jnp.dot(rw_ref[...], brw_ref[...], preferred_element_type=F32)
         + gate_ref[:, d:] * jnp.dot(att_ref[...], bra_ref[...], preferred_element_type=F32))
    o_ref[...] = x_ref[...] + gt_ref[0] * jnp.dot(m.astype(BF16), wo_ref[...], preferred_element_type=F32)


def _merge(rw, att, gates, x, gt, brw, bra, wo, tm, tps):
    rows, d = x.shape
    row = lambda i: (i, 0)
    const = lambda i: (0, 0)
    return pl.pallas_call(
        _merge_kernel,
        grid=(rows // tm,),
        in_specs=[
            pl.BlockSpec((tm, rw.shape[1]), row),
            pl.BlockSpec((tm, att.shape[1]), row),
            pl.BlockSpec((tm, gates.shape[1]), row),
            pl.BlockSpec((tm, d), row),
            pl.BlockSpec((1, gt.shape[1], d), lambda i: (i // tps, 0, 0)),
            pl.BlockSpec(brw.shape, const),
            pl.BlockSpec(bra.shape, const),
            pl.BlockSpec(wo.shape, const),
        ],
        out_specs=pl.BlockSpec((tm, d), row),
        out_shape=jax.ShapeDtypeStruct((rows, d), F32),
        compiler_params=_cparams("parallel"),
        name="merge_out",
    )(rw, att, gates, x, gt, brw, bra, wo)


def _mlp_kernel(fc, x_ref, sc_ref, sh_ref, gt_ref, g_ref, w1_ref, w2_ref, o_ref):
    x = x_ref[...]
    hb = _rms_mod(x, g_ref[...], sc_ref[0], sh_ref[0]).astype(BF16)
    acc = jnp.zeros(x.shape, F32)
    for c in range(w1_ref.shape[1] // fc):
        u = jnp.dot(hb, w1_ref[:, c * fc:(c + 1) * fc], preferred_element_type=F32)
        u = jnp.square(jnp.maximum(u, 0.0)).astype(BF16)
        acc = acc + jnp.dot(u, w2_ref[c * fc:(c + 1) * fc, :], preferred_element_type=F32)
    o_ref[...] = x + gt_ref[0] * acc


def _mlp(x, mods, g, w1, w2, tm, tps):
    rows, d = x.shape
    sc, sh, gt = mods
    r_mod = sc.shape[1]
    row = lambda i: (i, 0)
    const = lambda i: (0, 0)
    modspec = pl.BlockSpec((1, r_mod, d), lambda i: (i // tps, 0, 0))
    return pl.pallas_call(
        functools.partial(_mlp_kernel, 1024),
        grid=(rows // tm,),
        in_specs=[pl.BlockSpec((tm, d), row), modspec, modspec, modspec, pl.BlockSpec((1, d), const),
                  pl.BlockSpec(w1.shape, const, pipeline_mode=pl.Buffered(1)),
                  pl.BlockSpec(w2.shape, const, pipeline_mode=pl.Buffered(1))],
        out_specs=pl.BlockSpec((tm, d), row),
        out_shape=jax.ShapeDtypeStruct((rows, d), F32),
        compiler_params=_cparams("parallel"),
        name="mlp",
    )(x, sc, sh, gt, g, w1, w2)


def _rope_tables(pos):
    half = ROT_DIM // 2
    inv = ROPE_THETA ** (-jnp.arange(half, dtype=F32) / half)
    ang = pos.astype(F32)[:, None] * inv[None, :]
    cos, sin = jnp.cos(ang), jnp.sin(ang)
    t = pos.shape[0]
    ones = jnp.ones((t, A_DK - ROT_DIM), F32)
    zeros_h = jnp.zeros((t, half), F32)
    zeros_r = jnp.zeros((t, A_DK - ROT_DIM), F32)
    c = jnp.concatenate([cos, cos, ones], axis=1)
    s1 = jnp.concatenate([zeros_h, sin, zeros_r], axis=1)
    s2 = jnp.concatenate([-sin, zeros_h, zeros_r], axis=1)
    rep = lambda a: jnp.concatenate([a] * (LANES // A_DK), axis=1)
    return rep(c), rep(s1), rep(s2)


def _block_ones2(n, blk):
    i = jnp.arange(n) // blk
    e = (i[:, None] == i[None, :]).astype(BF16)
    return jnp.concatenate([e, e], axis=0)


def _split_mod(mod, per_row_reps):
    parts = jnp.split(mod, 6, axis=-1)
    if per_row_reps is None:
        return [m[:, None, :] for m in parts]
    return [jnp.repeat(m, per_row_reps, axis=0)[None] for m in parts]


def kernel(x_prompt, x_sample, c_prompt, c_sample, cache_k, cache_v, state_wkv, state_shift, page_table, norm1_g,
           norm2_g, ada_w, ada_b, w_in, shift_mu, rw_w0, rw_wb, rw_a0, rw_ab, rw_gb, rw_v0, rw_va, rw_vb, rw_kk,
           rw_ka, rw_rk, rw_ln_w, rw_ln_b, at_qnorm, at_knorm, at_lq1, at_lk1, at_lq2, at_lk2, at_subln, br_rwkv,
           br_attn, w_out, mlp_w1, mlp_w2):
    bp, tp, d = x_prompt.shape
    bs, ts, _ = x_sample.shape
    depth = w_in.shape[0]
    rw = rw_w0.shape[1]
    shift_cols = shift_mu.shape[1]
    qk_w = A_HEADS * 2 * A_DK
    v_w = A_HEADS * 2 * A_DK
    n_phys = cache_k.shape[1]
    n_pages = page_table.shape[1]
    past_len = n_pages * PAGE_SIZE
    r_heads = rw // R_HEAD

    mod_all = _ada_mod(jnp.concatenate([c_prompt, c_sample], axis=0), ada_w, ada_b)
    rope_p = _rope_tables(jnp.arange(tp))
    rope_s = tuple(jnp.tile(a, (bs, 1)) for a in _rope_tables(past_len + jnp.arange(ts)))
    e_qk = _block_ones2(qk_w, A_DK)
    e_rw = _block_ones2(rw, R_HEAD)
    ckt = jnp.transpose(cache_k, (0, 1, 3, 4, 5, 2)).reshape(depth * n_phys, qk_w, PAGE_SIZE)
    cv = cache_v.reshape(depth * n_phys, PAGE_SIZE * A_HEADS, v_w // A_HEADS)
    tpad = SUBLANES
    lane = jnp.arange(qk_w)
    q_own = ((lane // LANES)[None, None, :] == jnp.arange(A_HEADS)[:, None, None]) & (
        ((lane // A_DK) % 2)[None, None, :] == jnp.arange(2)[None, :, None])

    xp = x_prompt.reshape(bp * tp, d)
    xs = x_sample.reshape(bs * ts, d)
    rows_s = bs * ts
    tm_p = 256
    outs = {n: [] for n in ("kp", "vp", "ks", "vs", "sp", "ss", "shp", "shs")}
    vfp = vfs = None
    for l in range(depth):
        lam_init = 0.8 - 0.6 * math.exp(-0.3 * l)
        lam = (jnp.exp(jnp.sum(at_lq1[l] * at_lk1[l]).astype(F32)) - jnp.exp(jnp.sum(at_lq2[l] * at_lk2[l]).astype(F32))
               + lam_init).reshape(1)
        pad_w = jnp.zeros((A_RANK, rw), F32)
        p = dict(
            shift_mu=shift_mu[l][None], rw_w0=rw_w0[l][None], rw_a0=rw_a0[l][None],
            wb_pad=jnp.concatenate([rw_wb[l], pad_w], axis=0).astype(BF16),
            ab_pad=jnp.concatenate([jnp.zeros((W_RANK, rw), F32), rw_ab[l]], axis=0).astype(BF16),
            rw_gb=rw_gb[l].astype(BF16), rw_kk=rw_kk[l][None], rw_ka=rw_ka[l][None])
        if l > 0:
            p.update(rw_v0=rw_v0[l - 1][None], rw_va=rw_va[l - 1].astype(BF16), rw_vb=rw_vb[l - 1].astype(BF16))
        w_in_b = w_in[l].astype(BF16)
        qn = jnp.tile(at_qnorm[l], qk_w // A_DK)[None]
        kn = jnp.tile(at_knorm[l], qk_w // A_DK)[None]
        subln = at_subln[l][None]
        rk = rw_rk[l].reshape(1, rw)
        lnw, lnb = rw_ln_w[l][None], rw_ln_b[l][None]
        brw, bra, wo = br_rwkv[l].astype(BF16), br_attn[l].astype(BF16), w_out[l].astype(BF16)
        w1, w2 = mlp_w1[l].astype(BF16), mlp_w2[l].astype(BF16)
        g1, g2 = norm1_g[l][None], norm2_g[l][None]
        scale_out = 1.0 - lam_init

        sh1, sc1, gt1, sh2, sc2, gt2 = _split_mod(mod_all[l, :bp], None)
        tps = tp // tm_p
        zs, q, k, kb, v, vb, gates = _inproj(xp, (sc1, sh1), g1, w_in_b, qn, kn, e_qk, rope_p, shift_cols, qk_w, v_w,
                                             tm_p, tps)
        att = _attn_prompt(q.reshape(bp, tp, qk_w), kb.reshape(bp, tp, qk_w), vb.reshape(bp, tp, v_w), lam, subln,
                           scale_out, 128, 512).reshape(bp * tp, v_w)
        first = jnp.zeros((bp, 1, shift_cols), F32)
        r_, lw_, k_, v_, kp_, b_, g_ = _rwkv_prep(zs, zs, first, p, e_rw, vfp, True, tm_p, tps)
        if l == 0:
            vfp = v_
        sh3 = lambda a: a.reshape(bp, tp, rw)
        z0 = jnp.zeros((bp, r_heads // 2, LANES, LANES), F32)
        rwo, zfin = _wkv(sh3(r_), sh3(lw_), sh3(k_), sh3(v_), sh3(kp_), sh3(b_), sh3(g_), rk, lnw, lnb, z0)
        x1 = _merge(rwo.reshape(bp * tp, rw), att, gates, xp, gt1, brw, bra, wo, 512, tp // 512)
        xp = _mlp(x1, (sc2, sh2, gt2), g2, w1, w2, 512, tp // 512)
        outs["kp"].append(k.reshape(bp, tp, A_HEADS, 2, A_DK))
        outs["vp"].append(v.reshape(bp, tp, A_HEADS, 2 * A_DK))
        outs["sp"].append(_blockdiag_to_state(zfin))
        outs["shp"].append(zs.reshape(bp, tp, shift_cols)[:, -1])

        sh1, sc1, gt1, sh2, sc2, gt2 = _split_mod(mod_all[l, bp:], ts)
        zs, q, k, kb, v, vb, gates = _inproj(xs, (sc1, sh1), g1, w_in_b, qn, kn, e_qk, rope_s, shift_cols, qk_w, v_w,
                                             rows_s, 1)
        q3 = jnp.pad(q.reshape(bs, ts, qk_w), ((0, 0), (0, tpad - ts), (0, 0)))
        qbig = jnp.where(q_own[None, :, :, None, :], q3[:, None, None], 0).reshape(bs, 2 * A_HEADS * tpad, qk_w)
        pad_new = lambda a: jnp.pad(a.reshape(bs, ts, -1), ((0, 0), (0, PAGE_SIZE - ts), (0, 0)))
        att = _attn_decode(qbig, jnp.swapaxes(pad_new(k), 1, 2), pad_new(v), ckt, cv, page_table + l * n_phys, lam,
                           subln, scale_out)
        att = att[:, :ts].reshape(rows_s, v_w)
        zs3 = zs.reshape(bs, ts, shift_cols)
        prev = jnp.concatenate([state_shift[l][:, None, :], zs3[:, :-1]], axis=1).reshape(rows_s, shift_cols)
        r_, lw_, k_, v_, kp_, b_, g_ = _rwkv_prep(zs, prev, jnp.zeros((1, 1, shift_cols), F32), p, e_rw, vfs, False,
                                                  rows_s, 1)
        if l == 0:
            vfs = v_
        padc = lambda a: jnp.pad(a.reshape(bs, ts, rw), ((0, 0), (0, CHUNK - ts), (0, 0)))
        rwo, zfin = _wkv(padc(r_), padc(lw_), padc(k_), padc(v_), padc(kp_), padc(b_), padc(g_), rk, lnw, lnb,
                         _state_to_blockdiag(state_wkv[l]))
        rwo = rwo[:, :ts].reshape(rows_s, rw)
        x1 = _merge(rwo, att, gates, xs, gt1, brw, bra, wo, rows_s, 1)
        xs = _mlp(x1, (sc2, sh2, gt2), g2, w1, w2, rows_s, 1)
        outs["ks"].append(k.reshape(bs, ts, A_HEADS, 2, A_DK))
        outs["vs"].append(v.reshape(bs, ts, A_HEADS, 2 * A_DK))
        outs["ss"].append(_blockdiag_to_state(zfin))
        outs["shs"].append(zs3[:, -1])

    st = lambda n: jnp.stack(outs[n])
    return (xp.reshape(bp, tp, d), xs.reshape(bs, ts, d), st("kp"), st("vp"), st("ks"), st("vs"), st("sp"), st("ss"),
            st("shp"), st("shs"))
```

```python
import functools
import math

import jax
import jax.numpy as jnp
from jax import lax
from jax.experimental import pallas as pl
from jax.experimental.pallas import tpu as pltpu

F32, BF16 = jnp.float32, jnp.bfloat16

R_HEAD = 64
GN_EPS = 64e-5
A_HEADS = 4
A_DK = 64
ROT_DIM = A_DK // 4
ROPE_THETA = 500000.0
NORM_EPS = 1e-6
PAGE_SIZE = 128
W_RANK, A_RANK, G_RANK = 64, 64, 128

LANES = 128
SUBLANES = 8
VMEM_LIMIT = 56 * 1024 * 1024

CHUNK = 64
WKV_SEQS_PER_STEP = 4
TM_INPROJ, TM_PREP, TM_MERGE, TM_MLP = 512, 256, 512, 512
ATTN_TQ, ATTN_TK = 256, 512
PAGES_PER_STEP = 16

NN = (((1,), (0,)), ((), ()))
NT = (((1,), (1,)), ((), ()))


def _cparams(*sem):
    return pltpu.CompilerParams(dimension_semantics=sem, vmem_limit_bytes=VMEM_LIMIT)


def _dot(a, b, dims=NN):
    return lax.dot_general(a.astype(BF16), b.astype(BF16), dims, preferred_element_type=F32)


def _split2(x):
    hi = x.astype(BF16)
    lo = (x - hi.astype(F32)).astype(BF16)
    return hi, lo


def _mm3(a, b, dims=NN):
    a_hi, a_lo = _split2(a)
    b_hi, b_lo = _split2(b)
    return lax.dot_general(jnp.concatenate([a_hi, a_lo, a_hi], axis=1),
                           jnp.concatenate([b_hi, b_hi, b_lo], axis=dims[0][1][0]), dims,
                           preferred_element_type=F32)


def _seg(x, e2):
    return jnp.dot(jnp.concatenate(_split2(x), axis=1), e2, preferred_element_type=F32)


def _sigmoid(x):
    return 1.0 / (1.0 + jnp.exp(-x))


def _rms_mod(x, g, sc, sh):
    y = x * lax.rsqrt(jnp.mean(x * x, axis=-1, keepdims=True) + NORM_EPS) * g
    return y * (1.0 + sc) + sh


def _ada_kernel(c_ref, w_ref, b_ref, o_ref):
    c = c_ref[...]
    o_ref[0] = _dot(c * _sigmoid(c), w_ref[0]) + b_ref[0]


def _ada_mod(c_all, ada_w, ada_b):
    depth, d, n = ada_w.shape
    rows = c_all.shape[0]
    tn = d
    return pl.pallas_call(
        _ada_kernel,
        grid=(depth, n // tn),
        in_specs=[
            pl.BlockSpec((rows, d), lambda l, j: (0, 0)),
            pl.BlockSpec((1, d, tn), lambda l, j: (l, 0, j)),
            pl.BlockSpec((1, 1, tn), lambda l, j: (l, 0, j)),
        ],
        out_specs=pl.BlockSpec((1, rows, tn), lambda l, j: (l, 0, j)),
        out_shape=jax.ShapeDtypeStruct((depth, rows, n), F32),
        compiler_params=_cparams("parallel", "parallel"),
        name="ada_mod",
    )(c_all, ada_w, ada_b.reshape(depth, 1, n))


def _inproj_kernel(shift_cols, qk_w, v_w, x_ref, sc_ref, sh_ref, g_ref, w_ref, qn_ref, kn_ref, e_ref,
                   cos_ref, s1_ref, s2_ref, zs_ref, q_ref, k_ref, kb_ref, v_ref, vb_ref, gate_ref):
    hb = _rms_mod(x_ref[...], g_ref[...], sc_ref[0], sh_ref[0]).astype(BF16)
    o = 0
    zs_ref[...] = jnp.dot(hb, w_ref[:, o:o + shift_cols], preferred_element_type=F32)
    o += shift_cols
    reps = qk_w // LANES
    cos = jnp.concatenate([cos_ref[...]] * reps, axis=1)
    s1 = jnp.concatenate([s1_ref[...]] * reps, axis=1)
    s2 = jnp.concatenate([s2_ref[...]] * reps, axis=1)

    def norm_rope(z, gain):
        ms = _seg(z * z, e_ref[...]) * (1.0 / A_DK)
        y = z * lax.rsqrt(ms + NORM_EPS) * gain
        return y * cos + pltpu.roll(y, ROT_DIM // 2, 1) * s1 + pltpu.roll(y, qk_w - ROT_DIM // 2, 1) * s2

    q = norm_rope(jnp.dot(hb, w_ref[:, o:o + qk_w], preferred_element_type=F32), qn_ref[...])
    q_ref[...] = (q * (A_DK ** -0.5)).astype(BF16)
    o += qk_w
    k = norm_rope(jnp.dot(hb, w_ref[:, o:o + qk_w], preferred_element_type=F32), kn_ref[...])
    k_ref[...] = k
    kb_ref[...] = k.astype(BF16)
    o += qk_w
    v = jnp.dot(hb, w_ref[:, o:o + v_w], preferred_element_type=F32)
    v_ref[...] = v
    vb_ref[...] = v.astype(BF16)
    o += v_w
    gate_ref[...] = _sigmoid(jnp.dot(hb, w_ref[:, o:], preferred_element_type=F32))


def _inproj(x, mods, g, w_bf, qn, kn, e_qk, rope, shift_cols, qk_w, v_w, tm, tps):
    rows, d = x.shape
    in_cols = w_bf.shape[1]
    gate_w = in_cols - shift_cols - 2 * qk_w - v_w
    sc, sh = mods
    r_mod = sc.shape[1]
    row = lambda i: (i, 0)
    const = lambda i: (0, 0)
    modmap = lambda i: (i // tps, 0, 0)
    ropemap = lambda i: (i % tps, 0)
    outs = [(shift_cols, F32), (qk_w, BF16), (qk_w, F32), (qk_w, BF16), (v_w, F32), (v_w, BF16), (gate_w, F32)]
    return pl.pallas_call(
        functools.partial(_inproj_kernel, shift_cols, qk_w, v_w),
        grid=(rows // tm,),
        in_specs=[
            pl.BlockSpec((tm, d), row),
            pl.BlockSpec((1, r_mod, d), modmap),
            pl.BlockSpec((1, r_mod, d), modmap),
            pl.BlockSpec((1, d), const),
            pl.BlockSpec((d, in_cols), const, pipeline_mode=pl.Buffered(1)),
            pl.BlockSpec((1, qk_w), const),
            pl.BlockSpec((1, qk_w), const),
            pl.BlockSpec((2 * qk_w, qk_w), const),
            pl.BlockSpec((tm, LANES), ropemap),
            pl.BlockSpec((tm, LANES), ropemap),
            pl.BlockSpec((tm, LANES), ropemap),
        ],
        out_specs=[pl.BlockSpec((tm, w), row) for w, _ in outs],
        out_shape=[jax.ShapeDtypeStruct((rows, w), dt) for w, dt in outs],
        compiler_params=_cparams("parallel"),
        name="inproj",
    )(x, sc, sh, g, w_bf, qn, kn, e_qk, *rope)


def _prep_kernel(halo, has_vres, tps, rw, *refs):
    if has_vres:
        (zs_ref, prev_ref, first_ref, mu_ref, w0_ref, wb_ref, a0_ref, ab_ref, gb_ref, kk_ref, ka_ref, e_ref,
         v0_ref, va_ref, vb_ref, vf_ref, r_o, lw_o, k_o, v_o, kp_o, b_o, g_o) = refs
    else:
        (zs_ref, prev_ref, first_ref, mu_ref, w0_ref, wb_ref, a0_ref, ab_ref, gb_ref, kk_ref, ka_ref, e_ref,
         r_o, lw_o, k_o, v_o, kp_o, b_o, g_o) = refs
    z = zs_ref[...]
    if halo:
        tm = z.shape[0]
        rolled = pltpu.roll(z, 1, 0)
        first = jnp.where(pl.program_id(0) % tps == 0, first_ref[0], prev_ref[SUBLANES - 1:SUBLANES, :])
        prev = jnp.where(lax.broadcasted_iota(jnp.int32, (tm, 1), 0) == 0, first, rolled)
    else:
        prev = prev_ref[...]
    xs = z + (prev - z) * mu_ref[...]
    r = xs[:, 0:rw]
    k = xs[:, rw:2 * rw]
    v = xs[:, 2 * rw:3 * rw]
    zwa = xs[:, 3 * rw:3 * rw + W_RANK + A_RANK]
    zg = xs[:, 3 * rw + W_RANK + A_RANK:]
    u = -(w0_ref[...] + _dot(jnp.tanh(zwa), wb_ref[...]))
    softplus = jnp.maximum(u, 0.0) + jnp.log(1.0 + jnp.exp(-jnp.abs(u)))
    lw_o[...] = -jnp.exp(-softplus - 0.5)
    a = _sigmoid(a0_ref[...] + _dot(zwa, ab_ref[...]))
    g_o[...] = _dot(_sigmoid(zg), gb_ref[...])
    if has_vres:
        vg = _sigmoid(v0_ref[...] + _dot(_dot(v, va_ref[...]), vb_ref[...]))
        v = v + (vf_ref[...] - v) * vg
    kk = k * kk_ref[...]
    nrm = jnp.sqrt(_seg(kk * kk, e_ref[...]))
    kk = kk / jnp.maximum(nrm, 1e-12)
    r_o[...] = r
    k_o[...] = k * (1.0 + (a - 1.0) * ka_ref[...])
    v_o[...] = v
    kp_o[...] = kk
    b_o[...] = kk * a


def _rwkv_prep(zs, prev_src, first, p, e_rw, v_first, halo, tm, tps):
    rows, cols = zs.shape
    rw = p["rw_w0"].shape[1]
    has_vres = v_first is not None
    row = lambda i: (i, 0)
    const = lambda i: (0, 0)
    full = lambda a: pl.BlockSpec(a.shape, const)
    if halo:
        prev_spec = pl.BlockSpec((SUBLANES, cols), lambda i: (jnp.maximum(i * (tm // SUBLANES) - 1, 0), 0))
        first_spec = pl.BlockSpec((1, 1, cols), lambda i: (i // tps, 0, 0))
    else:
        prev_spec = pl.BlockSpec((tm, cols), row)
        first_spec = pl.BlockSpec((1, 1, cols), lambda i: (0, 0, 0))
    params = [p["shift_mu"], p["rw_w0"], p["wb_pad"], p["rw_a0"], p["ab_pad"], p["rw_gb"], p["rw_kk"], p["rw_ka"],
              e_rw]
    args = [zs, prev_src, first] + params
    in_specs = [pl.BlockSpec((tm, cols), row), prev_spec, first_spec] + [full(a) for a in params]
    if has_vres:
        extra = [p["rw_v0"], p["rw_va"], p["rw_vb"]]
        args += extra + [v_first]
        in_specs += [full(a) for a in extra] + [pl.BlockSpec((tm, rw), row)]
    return pl.pallas_call(
        functools.partial(_prep_kernel, halo, has_vres, tps, rw),
        grid=(rows // tm,),
        in_specs=in_specs,
        out_specs=[pl.BlockSpec((tm, rw), row)] * 7,
        out_shape=[jax.ShapeDtypeStruct((rows, rw), F32)] * 7,
        compiler_params=_cparams("parallel"),
        name="rwkv_prep",
    )(*args)


def _wkv_kernel(r_ref, lw_ref, k_ref, v_ref, kp_ref, b_ref, g_ref, rk_ref, lnw_ref, lnb_ref, z0_ref, y_ref, z_ref):
    C = CHUNK
    P2 = 2 * C

    @pl.when(pl.program_id(1) == 0)
    def _():
        z_ref[...] = z0_ref[...]

    ri = lax.broadcasted_iota(jnp.int32, (P2, P2), 0)
    ci = lax.broadcasted_iota(jnp.int32, (P2, P2), 1)
    strict = ci < ri
    incl = ci <= ri
    eye = (ci == ri).astype(F32)
    blk = lambda s: (ri >> s) == (ci >> s)
    e_pair = blk(6).astype(BF16)
    e_head = jnp.concatenate([e_pair, e_pair], axis=0)
    ti = lax.broadcasted_iota(jnp.int32, (C, C), 0)
    tj = lax.broadcasted_iota(jnp.int32, (C, C), 1)
    tril1 = (tj <= ti).astype(BF16)
    head_a = lax.broadcasted_iota(jnp.int32, (C, LANES), 1) < R_HEAD

    def stack(x):
        return jnp.concatenate([jnp.where(head_a, x, 0.0), jnp.where(head_a, 0.0, x)], axis=0)

    units = [(s, p) for s in range(r_ref.shape[0]) for p in range(r_ref.shape[2] // LANES)]
    pairs = range(len(units))
    sls = [slice(p * LANES, (p + 1) * LANES) for _, p in units]
    mm = _dot
    cat = jnp.concatenate
    ld = lambda ref: [ref[s, :, sl] for (s, _), sl in zip(units, sls)]
    r, lw, k, v, kp, b = ld(r_ref), ld(lw_ref), ld(k_ref), ld(v_ref), ld(kp_ref), ld(b_ref)

    def cumsum(x):
        x0 = x.astype(BF16)
        x1 = x - x0.astype(F32)
        x1b = x1.astype(BF16)
        x2b = (x1 - x1b.astype(F32)).astype(BF16)
        return jnp.dot(cat([tril1] * 3, axis=1), cat([x0, x1b, x2b], axis=0), preferred_element_type=F32)

    logp = [cumsum(x) for x in lw]
    logpc = [x[C - 1:C, :] for x in logp]
    e_neg = [jnp.exp(-x) for x in logp]
    e_end = [jnp.exp(xc - x) for xc, x in zip(logpc, logp)]
    rt = [stack(r[p] * jnp.exp(logp[p])) for p in pairs]
    kpt = [stack(kp[p] * jnp.exp(logp[p] - lw[p])) for p in pairs]
    v_s = [stack(v[p]) for p in pairs]
    gram = [_dot(cat([kpt[p], rt[p]], axis=0), cat([stack(b[p] * e_neg[p]), stack(k[p] * e_neg[p])], axis=0), NT)
            for p in pairs]
    lb = [jnp.where(strict, g[:P2, :P2], 0.0) for g in gram]
    lk = [jnp.where(strict, g[:P2, P2:], 0.0) for g in gram]
    mb = [jnp.where(incl, g[P2:, :P2], 0.0) for g in gram]
    mk = [jnp.where(incl, g[P2:, P2:], 0.0) for g in gram]
    x = [mm(lk[p], v_s[p]) for p in pairs]
    n1 = [jnp.where(blk(3), -m, 0.0) for m in lb]
    n2 = [mm(m, m) for m in n1]
    t = [mm(eye + a, eye + c) for a, c in zip(n1, n2)]
    n4 = [mm(m, m) for m in n2]
    t = [mm(a, eye + c) for a, c in zip(t, n4)]
    for s in (3, 4, 5):
        off_mask = blk(s + 1) & jnp.logical_not(blk(s))
        w = [mm(jnp.where(off_mask, lb[p], 0.0), t[p]) for p in pairs]
        t = [t[p] - mm(t[p], w[p]) for p in pairs]
    wu = [-mm(t[p], cat([kpt[p], x[p]], axis=1)) for p in pairs]
    lhs = [cat([cat([mb[p], mk[p]], axis=1),
                cat([stack(b[p] * e_end[p]).T, stack(k[p] * e_end[p]).T], axis=1)], axis=0) for p in pairs]
    rhs = [cat([wu[p], cat([jnp.zeros_like(v_s[p]), v_s[p]], axis=1)], axis=0) for p in pairs]
    big = [mm(lhs[p], rhs[p]) for p in pairs]
    qa = [cat([rt[p] + big[p][:P2, :LANES], eye * jnp.exp(logpc[p]) + big[p][P2:, :LANES]], axis=0) for p in pairs]
    qaz = [_mm3(qa[p], z_ref[units[p]]) for p in pairs]
    for p in pairs:
        z_ref[units[p]] = qaz[p][P2:] + big[p][P2:, LANES:]
    for p in pairs:
        seq = units[p][0]
        ys = qaz[p][:P2] + big[p][:P2, LANES:]
        y = ys[:C] + ys[C:]
        mu = _seg(y, e_head) * (1.0 / R_HEAD)
        d = y - mu
        var = _seg(d * d, e_head) * (1.0 / R_HEAD)
        yn = d * lax.rsqrt(var + GN_EPS) * lnw_ref[:, sls[p]] + lnb_ref[:, sls[p]]
        bonus = _seg(r[p] * k[p] * rk_ref[:, sls[p]], e_head) * v[p]
        y_ref[seq, :, sls[p]] = ((yn + bonus) * g_ref[seq, :, sls[p]]).astype(y_ref.dtype)


def _wkv(r, lw, k, v, kp, b, g, rk, lnw, lnb, z0, nb):
    bsz, t, rw = r.shape
    assert bsz % nb == 0 and t % CHUNK == 0 and rw % LANES == 0, (r.shape, nb)
    n_pairs = rw // LANES
    seq = lambda bb, c: (bb, c, 0)
    const = lambda bb, c: (0, 0)
    zmap = lambda bb, c: (bb, 0, 0, 0)
    return pl.pallas_call(
        _wkv_kernel,
        grid=(bsz // nb, t // CHUNK),
        in_specs=[pl.BlockSpec((nb, CHUNK, rw), seq)] * 7 + [pl.BlockSpec((1, rw), const)] * 3
        + [pl.BlockSpec((nb, n_pairs, LANES, LANES), zmap)],
        out_specs=[pl.BlockSpec((nb, CHUNK, rw), seq), pl.BlockSpec((nb, n_pairs, LANES, LANES), zmap)],
        out_shape=[jax.ShapeDtypeStruct((bsz, t, rw), BF16), jax.ShapeDtypeStruct((bsz, n_pairs, LANES, LANES), F32)],
        compiler_params=_cparams("parallel", "arbitrary"),
        name="wkv",
    )(r, lw, k, v, kp, b, g, rk, lnw, lnb, z0)


def _state_to_blockdiag(s):
    bsz, h, n, _ = s.shape
    zt = jnp.swapaxes(s, -1, -2).reshape(bsz, h // 2, 2, n, n)
    zero = jnp.zeros_like(zt[:, :, 0])
    top = jnp.concatenate([zt[:, :, 0], zero], axis=-1)
    bot = jnp.concatenate([zero, zt[:, :, 1]], axis=-1)
    return jnp.concatenate([top, bot], axis=-2)


def _blockdiag_to_state(z):
    n = R_HEAD
    za = z[:, :, :n, :n]
    zb = z[:, :, n:, n:]
    s = jnp.stack([za, zb], axis=2)
    s = s.reshape(z.shape[0], -1, n, n)
    return jnp.swapaxes(s, -1, -2)


def _attn_kernel(scale_out, tk, lam_ref, q_ref, k_ref, v_ref, sg_ref, o_ref, m_ref, l_ref, acc_ref):
    tq = q_ref.shape[1]
    heads = range(q_ref.shape[2] // LANES)
    hsl = [slice(h * LANES, (h + 1) * LANES) for h in heads]
    qi = pl.program_id(1)
    jd = qi // (tk // tq)
    first = lax.broadcasted_iota(jnp.int32, (tq, LANES), 1) < A_DK
    q2 = []
    for h in heads:
        q = q_ref[0, :, hsl[h]]
        zero = jnp.zeros_like(q)
        q2.append(jnp.concatenate([jnp.where(first, q, zero), jnp.where(first, zero, q)], axis=0))
    m_ref[...] = jnp.full(m_ref.shape, -jnp.inf, F32)
    l_ref[...] = jnp.zeros(l_ref.shape, F32)
    acc_ref[...] = jnp.zeros(acc_ref.shape, F32)
    reps = tk // LANES

    def block(start, mask):
        s = [lax.dot_general(q2[h], k_ref[0, pl.ds(start, tk), hsl[h]], NT, preferred_element_type=F32)
             for h in heads]
        if mask is not None:
            s = [jnp.where(mask, x, -jnp.inf) for x in s]
        m_prev = [m_ref[h] for h in heads]
        m_next = [jnp.maximum(m_prev[h], jnp.max(s[h], axis=1, keepdims=True)) for h in heads]
        p = [jnp.exp(s[h] - jnp.concatenate([m_next[h]] * reps, axis=1)) for h in heads]
        alpha = [jnp.exp(m_prev[h] - m_next[h]) for h in heads]
        for h in heads:
            l_ref[h] = alpha[h] * l_ref[h] + jnp.sum(p[h], axis=1, keepdims=True)
            m_ref[h] = m_next[h]
        pv = [jnp.dot(p[h].astype(BF16), v_ref[0, pl.ds(start, tk), hsl[h]], preferred_element_type=F32)
              for h in heads]
        for h in heads:
            acc_ref[h] = alpha[h] * acc_ref[h] + pv[h]

    def body(j, carry):
        block(pl.multiple_of(j * tk, tk), None)
        return carry

    lax.fori_loop(0, jd, body, 0)
    row = lax.broadcasted_iota(jnp.int32, (2 * tq, tk), 0)
    col = lax.broadcasted_iota(jnp.int32, (2 * tq, tk), 1)
    row = jnp.where(row >= tq, row - tq, row) + (qi * tq - jd * tk)
    block(pl.multiple_of(jd * tk, tk), col <= row)
    for h in heads:
        o = acc_ref[h] / l_ref[h]
        att = o[:tq] - lam_ref[0] * o[tq:]
        att = att * lax.rsqrt(jnp.mean(att * att, axis=-1, keepdims=True) + NORM_EPS) * sg_ref[...]
        o_ref[0, :, hsl[h]] = (att * scale_out).astype(o_ref.dtype)


def _attn_prompt(q, k, v, lam, subln, scale_out, tq, tk):
    bsz, t, w = q.shape
    heads = w // LANES
    return pl.pallas_call(
        functools.partial(_attn_kernel, scale_out, tk),
        grid=(bsz, t // tq),
        in_specs=[
            pl.BlockSpec(memory_space=pltpu.SMEM),
            pl.BlockSpec((1, tq, w), lambda b, i: (b, i, 0)),
            pl.BlockSpec((1, t, w), lambda b, i: (b, 0, 0)),
            pl.BlockSpec((1, t, w), lambda b, i: (b, 0, 0)),
            pl.BlockSpec((1, LANES), lambda b, i: (0, 0)),
        ],
        out_specs=pl.BlockSpec((1, tq, w), lambda b, i: (b, i, 0)),
        out_shape=jax.ShapeDtypeStruct((bsz, t, w), BF16),
        scratch_shapes=[pltpu.VMEM((heads, 2 * tq, LANES), F32)] * 3,
        compiler_params=_cparams("parallel", "arbitrary"),
        name="attn_prompt",
    )(lam, q, k, v, subln)


def _decode_kernel(scale_out, n_pages_step, pt_ref, lam_ref, q_ref, kn_ref, vn_ref, *refs):
    k_refs = refs[:n_pages_step]
    v_refs = refs[n_pages_step:2 * n_pages_step]
    sg_ref, o_ref, m_ref, l_ref, acc_ref = refs[2 * n_pages_step:]
    j = pl.program_id(1)
    rows = q_ref.shape[1]
    hrows = rows // A_HEADS
    tpad = hrows // 2

    @pl.when(j == 0)
    def _():
        m_ref[...] = jnp.full(m_ref.shape, -jnp.inf, F32)
        l_ref[...] = jnp.zeros(l_ref.shape, F32)
        acc_ref[...] = jnp.zeros(acc_ref.shape, F32)

    q = q_ref[0]

    def update(s_list, v_fn):
        m_prev = m_ref[...]
        m_cur = s_list[0].max(axis=1, keepdims=True)
        for s in s_list[1:]:
            m_cur = jnp.maximum(m_cur, s.max(axis=1, keepdims=True))
        m_next = jnp.maximum(m_prev, m_cur)
        alpha = jnp.exp(m_prev - m_next)
        l_new = alpha * l_ref[...]
        p_list = []
        for s in s_list:
            p = jnp.exp(s - m_next)
            l_new = l_new + jnp.sum(p, axis=1, keepdims=True)
            p_list.append(p.astype(BF16))
        l_ref[...] = l_new
        m_ref[...] = m_next
        for h in range(A_HEADS):
            hs = slice(h * hrows, (h + 1) * hrows)
            acc = alpha[hs] * acc_ref[hs, :]
            for i, p in enumerate(p_list):
                acc = acc + jnp.dot(p[hs], v_fn(i, h).astype(BF16), preferred_element_type=F32)
            acc_ref[hs, :] = acc

    s_list = [jnp.dot(q, kr[0].astype(BF16), preferred_element_type=F32) for kr in k_refs]
    update(s_list, lambda i, h: v_refs[i][0, pl.ds(h, PAGE_SIZE, stride=A_HEADS), :])

    @pl.when(j == pl.num_programs(1) - 1)
    def _():
        s = jnp.dot(q, kn_ref[0].astype(BF16), preferred_element_type=F32)
        tok = lax.broadcasted_iota(jnp.int32, s.shape, 0) & (tpad - 1)
        key = lax.broadcasted_iota(jnp.int32, s.shape, 1)
        update([jnp.where(key <= tok, s, -jnp.inf)], lambda i, h: vn_ref[0, :, h * LANES:(h + 1) * LANES])
        inv_l = 1.0 / l_ref[...]
        for h in range(A_HEADS):
            r0 = slice(h * hrows, h * hrows + tpad)
            r1 = slice(h * hrows + tpad, (h + 1) * hrows)
            att = acc_ref[r0, :] * inv_l[r0] - lam_ref[0] * (acc_ref[r1, :] * inv_l[r1])
            att = att * lax.rsqrt(jnp.mean(att * att, axis=-1, keepdims=True) + NORM_EPS) * sg_ref[...]
            o_ref[0, :, h * LANES:(h + 1) * LANES] = (att * scale_out).astype(o_ref.dtype)


def _attn_decode(qbig, kt_new, v_new, cache_kt, cache_v, page_ids, lam, subln, scale_out):
    bsz, rows, w = qbig.shape
    n_pages = page_ids.shape[1]
    pps = PAGES_PER_STEP
    tpad = rows // (2 * A_HEADS)

    def page_spec(i):
        return pl.BlockSpec((1, w, PAGE_SIZE), lambda b, j, pt: (pt[b, j * pps + i], 0, 0))

    per_b = lambda b, j, pt: (b, 0, 0)
    grid_spec = pltpu.PrefetchScalarGridSpec(
        num_scalar_prefetch=1,
        grid=(bsz, n_pages // pps),
        in_specs=[
            pl.BlockSpec(memory_space=pltpu.SMEM),
            pl.BlockSpec((1, rows, w), per_b),
            pl.BlockSpec((1, w, PAGE_SIZE), per_b),
            pl.BlockSpec((1, PAGE_SIZE, w), per_b),
        ] + [page_spec(i) for i in range(pps)] + [page_spec(i) for i in range(pps)]
        + [pl.BlockSpec((1, LANES), lambda b, j, pt: (0, 0))],
        out_specs=pl.BlockSpec((1, tpad, w), per_b),
        scratch_shapes=[pltpu.VMEM((rows, LANES), F32)] * 3,
    )
    return pl.pallas_call(
        functools.partial(_decode_kernel, scale_out, pps),
        grid_spec=grid_spec,
        out_shape=jax.ShapeDtypeStruct((bsz, tpad, w), BF16),
        compiler_params=_cparams("parallel", "arbitrary"),
        name="attn_decode",
    )(page_ids, lam, qbig, kt_new, v_new, *([cache_kt] * pps), *([cache_v] * pps), subln)


def _merge_kernel(rw_ref, att_ref, gate_ref, x_ref, gt_ref, brw_ref, bra_ref, wo_ref, o_ref):
    d = x_ref.shape[1]
    m = (gate_ref[:, :d] * jnp.dot(rw_ref[...], brw_ref[...], preferred_element_type=F32)
         + gate_ref[:, d:] * jnp.dot(att_ref[...], bra_ref[...], preferred_element_type=F32))
    o_ref[...] = x_ref[...] + gt_ref[0] * jnp.dot(m.astype(BF16), wo_ref[...], preferred_element_type=F32)


def _merge(rw, att, gates, x, gt, brw, bra, wo, tm, tps):
    rows, d = x.shape
    row = lambda i: (i, 0)
    const = lambda i: (0, 0)
    return pl.pallas_call(
        _merge_kernel,
        grid=(rows // tm,),
        in_specs=[
            pl.BlockSpec((tm, rw.shape[1]), row),
            pl.BlockSpec((tm, att.shape[1]), row),
            pl.BlockSpec((tm, gates.shape[1]), row),
            pl.BlockSpec((tm, d), row),
            pl.BlockSpec((1, gt.shape[1], d), lambda i: (i // tps, 0, 0)),
            pl.BlockSpec(brw.shape, const),
            pl.BlockSpec(bra.shape, const),
            pl.BlockSpec(wo.shape, const),
        ],
        out_specs=pl.BlockSpec((tm, d), row),
        out_shape=jax.ShapeDtypeStruct((rows, d), F32),
        compiler_params=_cparams("parallel"),
        name="merge_out",
    )(rw, att, gates, x, gt, brw, bra, wo)


def _mlp_kernel(fc, x_ref, sc_ref, sh_ref, gt_ref, g_ref, w1_ref, w2_ref, o_ref):
    x = x_ref[...]
    hb = _rms_mod(x, g_ref[...], sc_ref[0], sh_ref[0]).astype(BF16)
    acc = jnp.zeros(x.shape, F32)
    for c in range(w1_ref.shape[1] // fc):
        u = jnp.dot(hb, w1_ref[:, c * fc:(c + 1) * fc], preferred_element_type=F32)
        u = jnp.square(jnp.maximum(u, 0.0)).astype(BF16)
        acc = acc + jnp.dot(u, w2_ref[c * fc:(c + 1) * fc, :], preferred_element_type=F32)
    o_ref[...] = x + gt_ref[0] * acc


def _mlp(x, mods, g, w1, w2, tm, tps):
    rows, d = x.shape
    sc, sh, gt = mods
    r_mod = sc.shape[1]
    row = lambda i: (i, 0)
    const = lambda i: (0, 0)
    modspec = pl.BlockSpec((1, r_mod, d), lambda i: (i // tps, 0, 0))
    return pl.pallas_call(
        functools.partial(_mlp_kernel, 1024),
        grid=(rows // tm,),
        in_specs=[pl.BlockSpec((tm, d), row), modspec, modspec, modspec, pl.BlockSpec((1, d), const),
                  pl.BlockSpec(w1.shape, const, pipeline_mode=pl.Buffered(1)),
                  pl.BlockSpec(w2.shape, const, pipeline_mode=pl.Buffered(1))],
        out_specs=pl.BlockSpec((tm, d), row),
        out_shape=jax.ShapeDtypeStruct((rows, d), F32),
        compiler_params=_cparams("parallel"),
        name="mlp",
    )(x, sc, sh, gt, g, w1, w2)


def _rope_tables(pos):
    half = ROT_DIM // 2
    inv = ROPE_THETA ** (-jnp.arange(half, dtype=F32) / half)
    ang = pos.astype(F32)[:, None] * inv[None, :]
    cos, sin = jnp.cos(ang), jnp.sin(ang)
    t = pos.shape[0]
    ones = jnp.ones((t, A_DK - ROT_DIM), F32)
    zeros_h = jnp.zeros((t, half), F32)
    zeros_r = jnp.zeros((t, A_DK - ROT_DIM), F32)
    c = jnp.concatenate([cos, cos, ones], axis=1)
    s1 = jnp.concatenate([zeros_h, sin, zeros_r], axis=1)
    s2 = jnp.concatenate([-sin, zeros_h, zeros_r], axis=1)
    rep = lambda a: jnp.concatenate([a] * (LANES // A_DK), axis=1)
    return rep(c), rep(s1), rep(s2)


def _block_ones2(n, blk):
    i = jnp.arange(n) // blk
    e = (i[:, None] == i[None, :]).astype(BF16)
    return jnp.concatenate([e, e], axis=0)


def _split_mod(mod, per_row_reps):
    parts = jnp.split(mod, 6, axis=-1)
    if per_row_reps is None:
        return [m[:, None, :] for m in parts]
    return [jnp.repeat(m, per_row_reps, axis=0)[None] for m in parts]


def kernel(x_prompt, x_sample, c_prompt, c_sample, cache_k, cache_v, state_wkv, state_shift, page_table, norm1_g,
           norm2_g, ada_w, ada_b, w_in, shift_mu, rw_w0, rw_wb, rw_a0, rw_ab, rw_gb, rw_v0, rw_va, rw_vb, rw_kk,
           rw_ka, rw_rk, rw_ln_w, rw_ln_b, at_qnorm, at_knorm, at_lq1, at_lk1, at_lq2, at_lk2, at_subln, br_rwkv,
           br_attn, w_out, mlp_w1, mlp_w2):
    bp, tp, d = x_prompt.shape
    bs, ts, _ = x_sample.shape
    depth = w_in.shape[0]
    rw = rw_w0.shape[1]
    shift_cols = shift_mu.shape[1]
    qk_w = A_HEADS * 2 * A_DK
    v_w = A_HEADS * 2 * A_DK
    n_phys = cache_k.shape[1]
    n_pages = page_table.shape[1]
    past_len = n_pages * PAGE_SIZE
    r_heads = rw // R_HEAD

    mod_all = _ada_mod(jnp.concatenate([c_prompt, c_sample], axis=0), ada_w, ada_b)
    rope_p = _rope_tables(jnp.arange(tp))
    rope_s = tuple(jnp.tile(a, (bs, 1)) for a in _rope_tables(past_len + jnp.arange(ts)))
    e_qk = _block_ones2(qk_w, A_DK)
    e_rw = _block_ones2(rw, R_HEAD)
    ckt = jnp.transpose(cache_k, (0, 1, 3, 4, 5, 2)).reshape(depth * n_phys, qk_w, PAGE_SIZE)
    cv = cache_v.reshape(depth * n_phys, PAGE_SIZE * A_HEADS, v_w // A_HEADS)
    tpad = SUBLANES
    lane = jnp.arange(qk_w)
    q_own = ((lane // LANES)[None, None, :] == jnp.arange(A_HEADS)[:, None, None]) & (
        ((lane // A_DK) % 2)[None, None, :] == jnp.arange(2)[None, :, None])

    xp = x_prompt.reshape(bp * tp, d)
    xs = x_sample.reshape(bs * ts, d)
    rows_s = bs * ts
    outs = {n: [] for n in ("kp", "vp", "ks", "vs", "sp", "ss", "shp", "shs")}
    vfp = vfs = None
    for l in range(depth):
        lam_init = 0.8 - 0.6 * math.exp(-0.3 * l)
        lam = (jnp.exp(jnp.sum(at_lq1[l] * at_lk1[l]).astype(F32)) - jnp.exp(jnp.sum(at_lq2[l] * at_lk2[l]).astype(F32))
               + lam_init).reshape(1)
        pad_w = jnp.zeros((A_RANK, rw), F32)
        p = dict(
            shift_mu=shift_mu[l][None], rw_w0=rw_w0[l][None], rw_a0=rw_a0[l][None],
            wb_pad=jnp.concatenate([rw_wb[l], pad_w], axis=0).astype(BF16),
            ab_pad=jnp.concatenate([jnp.zeros((W_RANK, rw), F32), rw_ab[l]], axis=0).astype(BF16),
            rw_gb=rw_gb[l].astype(BF16), rw_kk=rw_kk[l][None], rw_ka=rw_ka[l][None])
        if l > 0:
            p.update(rw_v0=rw_v0[l - 1][None], rw_va=rw_va[l - 1].astype(BF16), rw_vb=rw_vb[l - 1].astype(BF16))
        w_in_b = w_in[l].astype(BF16)
        qn = jnp.tile(at_qnorm[l], qk_w // A_DK)[None]
        kn = jnp.tile(at_knorm[l], qk_w // A_DK)[None]
        subln = at_subln[l][None]
        rk = rw_rk[l].reshape(1, rw)
        lnw, lnb = rw_ln_w[l][None], rw_ln_b[l][None]
        brw, bra, wo = br_rwkv[l].astype(BF16), br_attn[l].astype(BF16), w_out[l].astype(BF16)
        w1, w2 = mlp_w1[l].astype(BF16), mlp_w2[l].astype(BF16)
        g1, g2 = norm1_g[l][None], norm2_g[l][None]
        scale_out = 1.0 - lam_init

        sh1, sc1, gt1, sh2, sc2, gt2 = _split_mod(mod_all[l, :bp], None)
        zs, q, k, kb, v, vb, gates = _inproj(xp, (sc1, sh1), g1, w_in_b, qn, kn, e_qk, rope_p, shift_cols, qk_w, v_w,
                                             TM_INPROJ, tp // TM_INPROJ)
        att = _attn_prompt(q.reshape(bp, tp, qk_w), kb.reshape(bp, tp, qk_w), vb.reshape(bp, tp, v_w), lam, subln,
                           scale_out, ATTN_TQ, ATTN_TK).reshape(bp * tp, v_w)
        first = jnp.zeros((bp, 1, shift_cols), F32)
        r_, lw_, k_, v_, kp_, b_, g_ = _rwkv_prep(zs, zs, first, p, e_rw, vfp, True, TM_PREP, tp // TM_PREP)
        if l == 0:
            vfp = v_
        sh3 = lambda a: a.reshape(bp, tp, rw)
        z0 = jnp.zeros((bp, r_heads // 2, LANES, LANES), F32)
        rwo, zfin = _wkv(sh3(r_), sh3(lw_), sh3(k_), sh3(v_), sh3(kp_), sh3(b_), sh3(g_), rk, lnw, lnb, z0,
                         WKV_SEQS_PER_STEP)
        x1 = _merge(rwo.reshape(bp * tp, rw), att, gates, xp, gt1, brw, bra, wo, TM_MERGE, tp // TM_MERGE)
        xp = _mlp(x1, (sc2, sh2, gt2), g2, w1, w2, TM_MLP, tp // TM_MLP)
        outs["kp"].append(k.reshape(bp, tp, A_HEADS, 2, A_DK))
        outs["vp"].append(v.reshape(bp, tp, A_HEADS, 2 * A_DK))
        outs["sp"].append(_blockdiag_to_state(zfin))
        outs["shp"].append(zs.reshape(bp, tp, shift_cols)[:, -1])

        sh1, sc1, gt1, sh2, sc2, gt2 = _split_mod(mod_all[l, bp:], ts)
        zs, q, k, kb, v, vb, gates = _inproj(xs, (sc1, sh1), g1, w_in_b, qn, kn, e_qk, rope_s, shift_cols, qk_w, v_w,
                                             rows_s, 1)
        q3 = jnp.pad(q.reshape(bs, ts, qk_w), ((0, 0), (0, tpad - ts), (0, 0)))
        qbig = jnp.where(q_own[None, :, :, None, :], q3[:, None, None], 0).reshape(bs, 2 * A_HEADS * tpad, qk_w)
        pad_new = lambda a: jnp.pad(a.reshape(bs, ts, -1), ((0, 0), (0, PAGE_SIZE - ts), (0, 0)))
        att = _attn_decode(qbig, jnp.swapaxes(pad_new(k), 1, 2), pad_new(v), ckt, cv, page_table + l * n_phys, lam,
                           subln, scale_out)
        att = att[:, :ts].reshape(rows_s, v_w)
        zs3 = zs.reshape(bs, ts, shift_cols)
        prev = jnp.concatenate([state_shift[l][:, None, :], zs3[:, :-1]], axis=1).reshape(rows_s, shift_cols)
        r_, lw_, k_, v_, kp_, b_, g_ = _rwkv_prep(zs, prev, jnp.zeros((1, 1, shift_cols), F32), p, e_rw, vfs, False,
                                                  rows_s, 1)
        if l == 0:
            vfs = v_
        padc = lambda a: jnp.pad(a.reshape(bs, ts, rw), ((0, 0), (0, CHUNK - ts), (0, 0)))
        rwo, zfin = _wkv(padc(r_), padc(lw_), padc(k_), padc(v_), padc(kp_), padc(b_), padc(g_), rk, lnw, lnb,
                         _state_to_blockdiag(state_wkv[l]), WKV_SEQS_PER_STEP)
        rwo = rwo[:, :ts].reshape(rows_s, rw)
        x1 = _merge(rwo, att, gates, xs, gt1, brw, bra, wo, rows_s, 1)
        xs = _mlp(x1, (sc2, sh2, gt2), g2, w1, w2, rows_s, 1)
        outs["ks"].append(k.reshape(bs, ts, A_HEADS, 2, A_DK))
        outs["vs"].append(v.reshape(bs, ts, A_HEADS, 2 * A_DK))
        outs["ss"].append(_blockdiag_to_state(zfin))
        outs["shs"].append(zs3[:, -1])

    st = lambda n: jnp.stack(outs[n])
    return (xp.reshape(bp, tp, d), xs.reshape(bs, ts, d), st("kp"), st("vp"), st("ks"), st("vs"), st("sp"), st("ss"),
            st("shp"), st("shs"))
```

```python
import functools
import math

import jax
import jax.numpy as jnp
from jax import lax
from jax.experimental import pallas as pl
from jax.experimental.pallas import tpu as pltpu

F32, BF16 = jnp.float32, jnp.bfloat16

R_HEAD = 64
GN_EPS = 64e-5
A_HEADS = 4
A_DK = 64
ROT_DIM = A_DK // 4
ROPE_THETA = 500000.0
NORM_EPS = 1e-6
PAGE_SIZE = 128
W_RANK, A_RANK, G_RANK = 64, 64, 128

LANES = 128
SUBLANES = 8
VMEM_LIMIT = 56 * 1024 * 1024

CHUNK = 64
WKV_SEQS_PER_STEP = 4
TM_INPROJ, TM_MERGE, TM_MLP = 512, 512, 512
ATTN_TQ, ATTN_TK = 256, 512
PAGES_PER_STEP = 16

NN = (((1,), (0,)), ((), ()))
NT = (((1,), (1,)), ((), ()))


def _cparams(*sem):
    return pltpu.CompilerParams(dimension_semantics=sem, vmem_limit_bytes=VMEM_LIMIT)


def _dot(a, b, dims=NN):
    return lax.dot_general(a.astype(BF16), b.astype(BF16), dims, preferred_element_type=F32)


def _split2(x):
    hi = x.astype(BF16)
    lo = (x - hi.astype(F32)).astype(BF16)
    return hi, lo


def _mm3(a, b, dims=NN):
    a_hi, a_lo = _split2(a)
    b_hi, b_lo = _split2(b)
    return lax.dot_general(jnp.concatenate([a_hi, a_lo, a_hi], axis=1),
                           jnp.concatenate([b_hi, b_hi, b_lo], axis=dims[0][1][0]), dims,
                           preferred_element_type=F32)


def _seg(x, e2):
    return jnp.dot(jnp.concatenate(_split2(x), axis=1), e2, preferred_element_type=F32)


def _sigmoid(x):
    return 1.0 / (1.0 + jnp.exp(-x))


def _rms_mod(x, g, sc, sh):
    y = x * lax.rsqrt(jnp.mean(x * x, axis=-1, keepdims=True) + NORM_EPS) * g
    return y * (1.0 + sc) + sh


def _ada_kernel(c_ref, w_ref, b_ref, o_ref):
    c = c_ref[...]
    o_ref[0] = _dot(c * _sigmoid(c), w_ref[0]) + b_ref[0]


def _ada_mod(c_all, ada_w, ada_b):
    depth, d, n = ada_w.shape
    rows = c_all.shape[0]
    tn = d
    return pl.pallas_call(
        _ada_kernel,
        grid=(depth, n // tn),
        in_specs=[
            pl.BlockSpec((rows, d), lambda l, j: (0, 0)),
            pl.BlockSpec((1, d, tn), lambda l, j: (l, 0, j)),
            pl.BlockSpec((1, 1, tn), lambda l, j: (l, 0, j)),
        ],
        out_specs=pl.BlockSpec((1, rows, tn), lambda l, j: (l, 0, j)),
        out_shape=jax.ShapeDtypeStruct((depth, rows, n), F32),
        compiler_params=_cparams("parallel", "parallel"),
        name="ada_mod",
    )(c_all, ada_w, ada_b.reshape(depth, 1, n))


def _inproj_kernel(shift_cols, qk_w, v_w, stacked, *refs):
    x_ref, sc_ref, sh_ref, g_ref, w_ref, qn_ref, kn_ref, e_ref, cos_ref, s1_ref, s2_ref = refs[-18:-7]
    zs_ref, q_ref, k_ref, kb_ref, v_ref, vb_ref, gate_ref = refs[-7:]
    hb = _rms_mod(x_ref[...], g_ref[...], sc_ref[0], sh_ref[0]).astype(BF16)
    o = 0
    zs_ref[...] = jnp.dot(hb, w_ref[:, o:o + shift_cols], preferred_element_type=F32)
    o += shift_cols
    reps = qk_w // LANES
    cos = jnp.concatenate([cos_ref[...]] * reps, axis=1)
    s1 = jnp.concatenate([s1_ref[...]] * reps, axis=1)
    s2 = jnp.concatenate([s2_ref[...]] * reps, axis=1)

    def norm_rope(z, gain):
        ms = _seg(z * z, e_ref[...]) * (1.0 / A_DK)
        y = z * lax.rsqrt(ms + NORM_EPS) * gain
        return y * cos + pltpu.roll(y, ROT_DIM // 2, 1) * s1 + pltpu.roll(y, qk_w - ROT_DIM // 2, 1) * s2

    q = norm_rope(jnp.dot(hb, w_ref[:, o:o + qk_w], preferred_element_type=F32), qn_ref[...])
    q_ref[...] = (q * (A_DK ** -0.5)).astype(BF16)
    o += qk_w
    k = norm_rope(jnp.dot(hb, w_ref[:, o:o + qk_w], preferred_element_type=F32), kn_ref[...])
    if stacked:
        k_ref[0, 0] = k.T
    else:
        k_ref[...] = k
    kb_ref[...] = k.astype(BF16)
    o += qk_w
    v = jnp.dot(hb, w_ref[:, o:o + v_w], preferred_element_type=F32)
    if stacked:
        for h in range(v_w // LANES):
            v_ref[0, pl.ds(h, v.shape[0], stride=v_w // LANES), :] = v[:, h * LANES:(h + 1) * LANES]
    else:
        v_ref[...] = v
    vb_ref[...] = v.astype(BF16)
    o += v_w
    gate_ref[...] = _sigmoid(jnp.dot(hb, w_ref[:, o:], preferred_element_type=F32))


def _inproj(x, mods, g, w_bf, qn, kn, e_qk, rope, shift_cols, qk_w, v_w, tm, tps, stack=None):
    rows, d = x.shape
    in_cols = w_bf.shape[1]
    gate_w = in_cols - shift_cols - 2 * qk_w - v_w
    sc, sh = mods
    r_mod = sc.shape[1]
    row = lambda i: (i, 0)
    const = lambda i: (0, 0)
    modmap = lambda i: (i // tps, 0, 0)
    ropemap = lambda i: (i % tps, 0)
    outs = [(shift_cols, F32), (qk_w, BF16), (qk_w, F32), (qk_w, BF16), (v_w, F32), (v_w, BF16), (gate_w, F32)]
    out_specs = [pl.BlockSpec((tm, w), row) for w, _ in outs]
    out_shape = [jax.ShapeDtypeStruct((rows, w), dt) for w, dt in outs]
    extra_in, extra_specs, aliases = [], [], {}
    if stack is not None:
        layer, depth, prev = stack
        heads = v_w // LANES
        seq_len = tm * tps
        out_specs[2] = pl.BlockSpec((1, 1, qk_w, tm), lambda i: (layer, i // tps, 0, i % tps))
        out_shape[2] = jax.ShapeDtypeStruct((depth, rows // seq_len, qk_w, seq_len), F32)
        out_specs[4] = pl.BlockSpec((1, tm * heads, LANES), lambda i: (layer, i, 0))
        out_shape[4] = jax.ShapeDtypeStruct((depth, rows * heads, LANES), F32)
        if prev is not None:
            extra_in = list(prev)
            extra_specs = [pl.BlockSpec(memory_space=pl.ANY)] * 2
            aliases = {0: 2, 1: 4}
    return pl.pallas_call(
        functools.partial(_inproj_kernel, shift_cols, qk_w, v_w, stack is not None),
        grid=(rows // tm,),
        input_output_aliases=aliases,
        in_specs=extra_specs + [
            pl.BlockSpec((tm, d), row),
            pl.BlockSpec((1, r_mod, d), modmap),
            pl.BlockSpec((1, r_mod, d), modmap),
            pl.BlockSpec((1, d), const),
            pl.BlockSpec((d, in_cols), const, pipeline_mode=pl.Buffered(1)),
            pl.BlockSpec((1, qk_w), const),
            pl.BlockSpec((1, qk_w), const),
            pl.BlockSpec((2 * qk_w, qk_w), const),
            pl.BlockSpec((tm, LANES), ropemap),
            pl.BlockSpec((tm, LANES), ropemap),
            pl.BlockSpec((tm, LANES), ropemap),
        ],
        out_specs=out_specs,
        out_shape=out_shape,
        compiler_params=_cparams("parallel"),
        name="inproj",
    )(*extra_in, x, sc, sh, g, w_bf, qn, kn, e_qk, *rope)


def _rwkv_kernel(has_vres, valid, *refs):
    (zs_ref, first_ref, mu_ref, w0_ref, wb_ref, a0_ref, ab_ref, gb_ref, kk_ref, ka_ref, e_ref, rk_ref, lnw_ref,
     lnb_ref, z0_ref) = refs[:15]
    if has_vres:
        v0_ref, va_ref, vb_ref, vf_ref, y_ref, z_ref, carry_ref = refs[15:]
    else:
        y_ref, z_ref, vout_ref, carry_ref = refs[15:]
    C = CHUNK
    P2 = 2 * C
    nb, _, cols = zs_ref.shape
    rw = rk_ref.shape[1]
    chunk = pl.program_id(1)

    @pl.when(chunk == 0)
    def _():
        z_ref[...] = z0_ref[...]
        for s in range(nb):
            carry_ref[s] = jnp.broadcast_to(first_ref[s], (SUBLANES, cols))

    row_id = lax.broadcasted_iota(jnp.int32, (C, 1), 0)

    def prep(s):
        z = zs_ref[s]
        prev = jnp.where(row_id == 0, carry_ref[s, SUBLANES - 1:SUBLANES, :], pltpu.roll(z, 1, 0))
        carry_ref[s] = z[C - SUBLANES:, :]
        xs = z + (prev - z) * mu_ref[...]
        r = xs[:, 0:rw]
        k = xs[:, rw:2 * rw]
        v = xs[:, 2 * rw:3 * rw]
        zwa = xs[:, 3 * rw:3 * rw + W_RANK + A_RANK]
        zg = xs[:, 3 * rw + W_RANK + A_RANK:]
        u = -(w0_ref[...] + _dot(jnp.tanh(zwa), wb_ref[...]))
        softplus = jnp.maximum(u, 0.0) + jnp.log(1.0 + jnp.exp(-jnp.abs(u)))
        lw = -jnp.exp(-softplus - 0.5)
        a = _sigmoid(a0_ref[...] + _dot(zwa, ab_ref[...]))
        g = _dot(_sigmoid(zg), gb_ref[...])
        if has_vres:
            vg = _sigmoid(v0_ref[...] + _dot(_dot(v, va_ref[...]), vb_ref[...]))
            v = v + (vf_ref[s] - v) * vg
        else:
            vout_ref[s] = v
        kk = k * kk_ref[...]
        kk = kk / jnp.maximum(jnp.sqrt(_seg(kk * kk, e_ref[...])), 1e-12)
        k = k * (1.0 + (a - 1.0) * ka_ref[...])
        b = kk * a
        if valid < C:
            keep = row_id < valid
            lw, k, v, kk, b = (jnp.where(keep, t, 0.0) for t in (lw, k, v, kk, b))
        return r, lw, k, v, kk, b, g

    ri = lax.broadcasted_iota(jnp.int32, (P2, P2), 0)
    ci = lax.broadcasted_iota(jnp.int32, (P2, P2), 1)
    strict = ci < ri
    incl = ci <= ri
    eye = (ci == ri).astype(F32)
    blk = lambda s: (ri >> s) == (ci >> s)
    e_pair = blk(6).astype(BF16)
    e_head = jnp.concatenate([e_pair, e_pair], axis=0)
    ti = lax.broadcasted_iota(jnp.int32, (C, C), 0)
    tj = lax.broadcasted_iota(jnp.int32, (C, C), 1)
    tril1 = (tj <= ti).astype(BF16)
    head_a = lax.broadcasted_iota(jnp.int32, (C, LANES), 1) < R_HEAD

    def stack(x):
        return jnp.concatenate([jnp.where(head_a, x, 0.0), jnp.where(head_a, 0.0, x)], axis=0)

    units = [(s, p) for s in range(nb) for p in range(rw // LANES)]
    pairs = range(len(units))
    sls = [slice(p * LANES, (p + 1) * LANES) for _, p in units]
    mm = _dot
    cat = jnp.concatenate
    tok = [prep(s) for s in range(nb)]
    ld = lambda i: [tok[s][i][:, sl] for (s, _), sl in zip(units, sls)]
    r, lw, k, v, kp, b = (ld(i) for i in range(6))

    def cumsum(x):
        x0 = x.astype(BF16)
        x1 = x - x0.astype(F32)
        x1b = x1.astype(BF16)
        x2b = (x1 - x1b.astype(F32)).astype(BF16)
        return jnp.dot(cat([tril1] * 3, axis=1), cat([x0, x1b, x2b], axis=0), preferred_element_type=F32)

    logp = [cumsum(x) for x in lw]
    logpc = [x[C - 1:C, :] for x in logp]
    e_neg = [jnp.exp(-x) for x in logp]
    e_end = [jnp.exp(xc - x) for xc, x in zip(logpc, logp)]
    rt = [stack(r[p] * jnp.exp(logp[p])) for p in pairs]
    kpt = [stack(kp[p] * jnp.exp(logp[p] - lw[p])) for p in pairs]
    v_s = [stack(v[p]) for p in pairs]
    gram = [_dot(cat([kpt[p], rt[p]], axis=0), cat([stack(b[p] * e_neg[p]), stack(k[p] * e_neg[p])], axis=0), NT)
            for p in pairs]
    lb = [jnp.where(strict, g[:P2, :P2], 0.0) for g in gram]
    lk = [jnp.where(strict, g[:P2, P2:], 0.0) for g in gram]
    mb = [jnp.where(incl, g[P2:, :P2], 0.0) for g in gram]
    mk = [jnp.where(incl, g[P2:, P2:], 0.0) for g in gram]
    x = [mm(lk[p], v_s[p]) for p in pairs]
    n1 = [jnp.where(blk(3), -m, 0.0) for m in lb]
    n2 = [mm(m, m) for m in n1]
    t = [mm(eye + a, eye + c) for a, c in zip(n1, n2)]
    n4 = [mm(m, m) for m in n2]
    t = [mm(a, eye + c) for a, c in zip(t, n4)]
    for s in (3, 4, 5):
        off_mask = blk(s + 1) & jnp.logical_not(blk(s))
        w = [mm(jnp.where(off_mask, lb[p], 0.0), t[p]) for p in pairs]
        t = [t[p] - mm(t[p], w[p]) for p in pairs]
    wu = [-mm(t[p], cat([kpt[p], x[p]], axis=1)) for p in pairs]
    lhs = [cat([cat([mb[p], mk[p]], axis=1),
                cat([stack(b[p] * e_end[p]).T, stack(k[p] * e_end[p]).T], axis=1)], axis=0) for p in pairs]
    rhs = [cat([wu[p], cat([jnp.zeros_like(v_s[p]), v_s[p]], axis=1)], axis=0) for p in pairs]
    big = [mm(lhs[p], rhs[p]) for p in pairs]
    qa = [cat([rt[p] + big[p][:P2, :LANES], eye * jnp.exp(logpc[p]) + big[p][P2:, :LANES]], axis=0) for p in pairs]
    qaz = [_mm3(qa[p], z_ref[units[p]]) for p in pairs]
    for p in pairs:
        z_ref[units[p]] = qaz[p][P2:] + big[p][P2:, LANES:]
    for p in pairs:
        seq = units[p][0]
        ys = qaz[p][:P2] + big[p][:P2, LANES:]
        y = ys[:C] + ys[C:]
        mu = _seg(y, e_head) * (1.0 / R_HEAD)
        d = y - mu
        var = _seg(d * d, e_head) * (1.0 / R_HEAD)
        yn = d * lax.rsqrt(var + GN_EPS) * lnw_ref[:, sls[p]] + lnb_ref[:, sls[p]]
        bonus = _seg(r[p] * k[p] * rk_ref[:, sls[p]], e_head) * v[p]
        y_ref[seq, :, sls[p]] = ((yn + bonus) * tok[seq][6][:, sls[p]]).astype(y_ref.dtype)


def _rwkv(zs, first, p, e_rw, rk, lnw, lnb, z0, v_first, valid, nb):
    bsz, t, cols = zs.shape
    rw = rk.shape[1]
    assert bsz % nb == 0 and t % CHUNK == 0 and rw % LANES == 0, (zs.shape, nb)
    n_pairs = rw // LANES
    has_vres = v_first is not None
    seq = lambda bb, c: (bb, c, 0)
    per_b = lambda bb, c: (bb, 0, 0)
    const = lambda bb, c: (0, 0)
    zmap = lambda bb, c: (bb, 0, 0, 0)
    full = lambda a: pl.BlockSpec(a.shape, const)
    params = [p["shift_mu"], p["rw_w0"], p["wb_pad"], p["rw_a0"], p["ab_pad"], p["rw_gb"], p["rw_kk"], p["rw_ka"],
              e_rw, rk, lnw, lnb]
    args = [zs, first] + params + [z0]
    in_specs = ([pl.BlockSpec((nb, CHUNK, cols), seq), pl.BlockSpec((nb, 1, cols), per_b)] + [full(a) for a in params]
                + [pl.BlockSpec((nb, n_pairs, LANES, LANES), zmap)])
    out_specs = [pl.BlockSpec((nb, CHUNK, rw), seq), pl.BlockSpec((nb, n_pairs, LANES, LANES), zmap)]
    out_shape = [jax.ShapeDtypeStruct((bsz, t, rw), BF16), jax.ShapeDtypeStruct((bsz, n_pairs, LANES, LANES), F32)]
    if has_vres:
        extra = [p["rw_v0"], p["rw_va"], p["rw_vb"]]
        args += extra + [v_first]
        in_specs += [full(a) for a in extra] + [pl.BlockSpec((nb, CHUNK, rw), seq)]
    else:
        out_specs.append(pl.BlockSpec((nb, CHUNK, rw), seq))
        out_shape.append(jax.ShapeDtypeStruct((bsz, t, rw), F32))
    return pl.pallas_call(
        functools.partial(_rwkv_kernel, has_vres, valid),
        grid=(bsz // nb, t // CHUNK),
        in_specs=in_specs,
        out_specs=out_specs,
        out_shape=out_shape,
        scratch_shapes=[pltpu.VMEM((nb, SUBLANES, cols), F32)],
        compiler_params=_cparams("parallel", "arbitrary"),
        name="rwkv",
    )(*args)


def _state_to_blockdiag(s):
    bsz, h, n, _ = s.shape
    zt = jnp.swapaxes(s, -1, -2).reshape(bsz, h // 2, 2, n, n)
    zero = jnp.zeros_like(zt[:, :, 0])
    top = jnp.concatenate([zt[:, :, 0], zero], axis=-1)
    bot = jnp.concatenate([zero, zt[:, :, 1]], axis=-1)
    return jnp.concatenate([top, bot], axis=-2)


def _blockdiag_to_state(z):
    n = R_HEAD
    za = z[:, :, :n, :n]
    zb = z[:, :, n:, n:]
    s = jnp.stack([za, zb], axis=2)
    s = s.reshape(z.shape[0], -1, n, n)
    return jnp.swapaxes(s, -1, -2)


def _attn_kernel(scale_out, tk, lam_ref, q_ref, k_ref, v_ref, sg_ref, o_ref, m_ref, l_ref, acc_ref):
    tq = q_ref.shape[1]
    heads = range(q_ref.shape[2] // LANES)
    hsl = [slice(h * LANES, (h + 1) * LANES) for h in heads]
    qi = pl.program_id(1)
    jd = qi // (tk // tq)
    first = lax.broadcasted_iota(jnp.int32, (tq, LANES), 1) < A_DK
    q2 = []
    for h in heads:
        q = q_ref[0, :, hsl[h]]
        zero = jnp.zeros_like(q)
        q2.append(jnp.concatenate([jnp.where(first, q, zero), jnp.where(first, zero, q)], axis=0))
    m_ref[...] = jnp.full(m_ref.shape, -jnp.inf, F32)
    l_ref[...] = jnp.zeros(l_ref.shape, F32)
    acc_ref[...] = jnp.zeros(acc_ref.shape, F32)
    reps = tk // LANES

    def block(start, mask):
        s = [lax.dot_general(q2[h], k_ref[0, pl.ds(start, tk), hsl[h]], NT, preferred_element_type=F32)
             for h in heads]
        if mask is not None:
            s = [jnp.where(mask, x, -jnp.inf) for x in s]
        m_prev = [m_ref[h] for h in heads]
        m_next = [jnp.maximum(m_prev[h], jnp.max(s[h], axis=1, keepdims=True)) for h in heads]
        p = [jnp.exp(s[h] - jnp.concatenate([m_next[h]] * reps, axis=1)) for h in heads]
        alpha = [jnp.exp(m_prev[h] - m_next[h]) for h in heads]
        for h in heads:
            l_ref[h] = alpha[h] * l_ref[h] + jnp.sum(p[h], axis=1, keepdims=True)
            m_ref[h] = m_next[h]
        pv = [jnp.dot(p[h].astype(BF16), v_ref[0, pl.ds(start, tk), hsl[h]], preferred_element_type=F32)
              for h in heads]
        for h in heads:
            acc_ref[h] = alpha[h] * acc_ref[h] + pv[h]

    def body(j, carry):
        block(pl.multiple_of(j * tk, tk), None)
        return carry

    lax.fori_loop(0, jd, body, 0)
    row = lax.broadcasted_iota(jnp.int32, (2 * tq, tk), 0)
    col = lax.broadcasted_iota(jnp.int32, (2 * tq, tk), 1)
    row = jnp.where(row >= tq, row - tq, row) + (qi * tq - jd * tk)
    block(pl.multiple_of(jd * tk, tk), col <= row)
    for h in heads:
        o = acc_ref[h] / l_ref[h]
        att = o[:tq] - lam_ref[0] * o[tq:]
        att = att * lax.rsqrt(jnp.mean(att * att, axis=-1, keepdims=True) + NORM_EPS) * sg_ref[...]
        o_ref[0, :, hsl[h]] = (att * scale_out).astype(o_ref.dtype)


def _attn_prompt(q, k, v, lam, subln, scale_out, tq, tk):
    bsz, t, w = q.shape
    heads = w // LANES
    return pl.pallas_call(
        functools.partial(_attn_kernel, scale_out, tk),
        grid=(bsz, t // tq),
        in_specs=[
            pl.BlockSpec(memory_space=pltpu.SMEM),
            pl.BlockSpec((1, tq, w), lambda b, i: (b, i, 0)),
            pl.BlockSpec((1, t, w), lambda b, i: (b, 0, 0)),
            pl.BlockSpec((1, t, w), lambda b, i: (b, 0, 0)),
            pl.BlockSpec((1, LANES), lambda b, i: (0, 0)),
        ],
        out_specs=pl.BlockSpec((1, tq, w), lambda b, i: (b, i, 0)),
        out_shape=jax.ShapeDtypeStruct((bsz, t, w), BF16),
        scratch_shapes=[pltpu.VMEM((heads, 2 * tq, LANES), F32)] * 3,
        compiler_params=_cparams("parallel", "arbitrary"),
        name="attn_prompt",
    )(lam, q, k, v, subln)


def _decode_kernel(scale_out, n_pages_step, pt_ref, lam_ref, q_ref, kn_ref, vn_ref, *refs):
    k_refs = refs[:n_pages_step]
    v_refs = refs[n_pages_step:2 * n_pages_step]
    sg_ref, o_ref, m_ref, l_ref, acc_ref = refs[2 * n_pages_step:]
    j = pl.program_id(1)
    rows = q_ref.shape[1]
    hrows = rows // A_HEADS
    tpad = hrows // 2

    @pl.when(j == 0)
    def _():
        m_ref[...] = jnp.full(m_ref.shape, -jnp.inf, F32)
        l_ref[...] = jnp.zeros(l_ref.shape, F32)
        acc_ref[...] = jnp.zeros(acc_ref.shape, F32)

    q = q_ref[0]

    def update(s_list, v_fn):
        m_prev = m_ref[...]
        m_cur = s_list[0].max(axis=1, keepdims=True)
        for s in s_list[1:]:
            m_cur = jnp.maximum(m_cur, s.max(axis=1, keepdims=True))
        m_next = jnp.maximum(m_prev, m_cur)
        alpha = jnp.exp(m_prev - m_next)
        l_new = alpha * l_ref[...]
        p_list = []
        for s in s_list:
            p = jnp.exp(s - m_next)
            l_new = l_new + jnp.sum(p, axis=1, keepdims=True)
            p_list.append(p.astype(BF16))
        l_ref[...] = l_new
        m_ref[...] = m_next
        for h in range(A_HEADS):
            hs = slice(h * hrows, (h + 1) * hrows)
            acc = alpha[hs] * acc_ref[hs, :]
            for i, p in enumerate(p_list):
                acc = acc + jnp.dot(p[hs], v_fn(i, h).astype(BF16), preferred_element_type=F32)
            acc_ref[hs, :] = acc

    s_list = [jnp.dot(q, kr[0].astype(BF16), preferred_element_type=F32) for kr in k_refs]
    update(s_list, lambda i, h: v_refs[i][0, pl.ds(h, PAGE_SIZE, stride=A_HEADS), :])

    @pl.when(j == pl.num_programs(1) - 1)
    def _():
        s = jnp.dot(q, kn_ref[0].astype(BF16), preferred_element_type=F32)
        tok = lax.broadcasted_iota(jnp.int32, s.shape, 0) & (tpad - 1)
        key = lax.broadcasted_iota(jnp.int32, s.shape, 1)
        update([jnp.where(key <= tok, s, -jnp.inf)], lambda i, h: vn_ref[0, :, h * LANES:(h + 1) * LANES])
        inv_l = 1.0 / l_ref[...]
        for h in range(A_HEADS):
            r0 = slice(h * hrows, h * hrows + tpad)
            r1 = slice(h * hrows + tpad, (h + 1) * hrows)
            att = acc_ref[r0, :] * inv_l[r0] - lam_ref[0] * (acc_ref[r1, :] * inv_l[r1])
            att = att * lax.rsqrt(jnp.mean(att * att, axis=-1, keepdims=True) + NORM_EPS) * sg_ref[...]
            o_ref[0, :, h * LANES:(h + 1) * LANES] = (att * scale_out).astype(o_ref.dtype)


def _attn_decode(qbig, kt_new, v_new, cache_kt, cache_v, page_ids, lam, subln, scale_out):
    bsz, rows, w = qbig.shape
    n_pages = page_ids.shape[1]
    pps = PAGES_PER_STEP
    tpad = rows // (2 * A_HEADS)

    def page_spec(i):
        return pl.BlockSpec((1, w, PAGE_SIZE), lambda b, j, pt: (pt[b, j * pps + i], 0, 0))

    per_b = lambda b, j, pt: (b, 0, 0)
    grid_spec = pltpu.PrefetchScalarGridSpec(
        num_scalar_prefetch=1,
        grid=(bsz, n_pages // pps),
        in_specs=[
            pl.BlockSpec(memory_space=pltpu.SMEM),
            pl.BlockSpec((1, rows, w), per_b),
            pl.BlockSpec((1, w, PAGE_SIZE), per_b),
            pl.BlockSpec((1, PAGE_SIZE, w), per_b),
        ] + [page_spec(i) for i in range(pps)] + [page_spec(i) for i in range(pps)]
        + [pl.BlockSpec((1, LANES), lambda b, j, pt: (0, 0))],
        out_specs=pl.BlockSpec((1, tpad, w), per_b),
        scratch_shapes=[pltpu.VMEM((rows, LANES), F32)] * 3,
    )
    return pl.pallas_call(
        functools.partial(_decode_kernel, scale_out, pps),
        grid_spec=grid_spec,
        out_shape=jax.ShapeDtypeStruct((bsz, tpad, w), BF16),
        compiler_params=_cparams("parallel", "arbitrary"),
        name="attn_decode",
    )(page_ids, lam, qbig, kt_new, v_new, *([cache_kt] * pps), *([cache_v] * pps), subln)


def _merge_kernel(rw_ref, att_ref, gate_ref, x_ref, gt_ref, brw_ref, bra_ref, wo_ref, o_ref):
    d = x_ref.shape[1]
    m = (gate_ref[:, :d] * jnp.dot(rw_ref[...], brw_ref[...], preferred_element_type=F32)
         + gate_ref[:, d:] * jnp.dot(att_ref[...], bra_ref[...], preferred_element_type=F32))
    o_ref[...] = x_ref[...] + gt_ref[0] * jnp.dot(m.astype(BF16), wo_ref[...], preferred_element_type=F32)


def _merge(rw, att, gates, x, gt, brw, bra, wo, tm, tps):
    rows, d = x.shape
    row = lambda i: (i, 0)
    const = lambda i: (0, 0)
    return pl.pallas_call(
        _merge_kernel,
        grid=(rows // tm,),
        in_specs=[
            pl.BlockSpec((tm, rw.shape[1]), row),
            pl.BlockSpec((tm, att.shape[1]), row),
            pl.BlockSpec((tm, gates.shape[1]), row),
            pl.BlockSpec((tm, d), row),
            pl.BlockSpec((1, gt.shape[1], d), lambda i: (i // tps, 0, 0)),
            pl.BlockSpec(brw.shape, const),
            pl.BlockSpec(bra.shape, const),
            pl.BlockSpec(wo.shape, const),
        ],
        out_specs=pl.BlockSpec((tm, d), row),
        out_shape=jax.ShapeDtypeStruct((rows, d), F32),
        compiler_params=_cparams("parallel"),
        name="merge_out",
    )(rw, att, gates, x, gt, brw, bra, wo)


def _mlp_kernel(fc, x_ref, sc_ref, sh_ref, gt_ref, g_ref, w1_ref, w2_ref, o_ref):
    x = x_ref[...]
    hb = _rms_mod(x, g_ref[...], sc_ref[0], sh_ref[0]).astype(BF16)
    acc = jnp.zeros(x.shape, F32)
    for c in range(w1_ref.shape[1] // fc):
        u = jnp.dot(hb, w1_ref[:, c * fc:(c + 1) * fc], preferred_element_type=F32)
        u = jnp.square(jnp.maximum(u, 0.0)).astype(BF16)
        acc = acc + jnp.dot(u, w2_ref[c * fc:(c + 1) * fc, :], preferred_element_type=F32)
    o_ref[...] = x + gt_ref[0] * acc


def _mlp(x, mods, g, w1, w2, tm, tps):
    rows, d = x.shape
    sc, sh, gt = mods
    r_mod = sc.shape[1]
    row = lambda i: (i, 0)
    const = lambda i: (0, 0)
    modspec = pl.BlockSpec((1, r_mod, d), lambda i: (i // tps, 0, 0))
    return pl.pallas_call(
        functools.partial(_mlp_kernel, 1024),
        grid=(rows // tm,),
        in_specs=[pl.BlockSpec((tm, d), row), modspec, modspec, modspec, pl.BlockSpec((1, d), const),
                  pl.BlockSpec(w1.shape, const, pipeline_mode=pl.Buffered(1)),
                  pl.BlockSpec(w2.shape, const, pipeline_mode=pl.Buffered(1))],
        out_specs=pl.BlockSpec((tm, d), row),
        out_shape=jax.ShapeDtypeStruct((rows, d), F32),
        compiler_params=_cparams("parallel"),
        name="mlp",
    )(x, sc, sh, gt, g, w1, w2)


def _rope_tables(pos):
    half = ROT_DIM // 2
    inv = ROPE_THETA ** (-jnp.arange(half, dtype=F32) / half)
    ang = pos.astype(F32)[:, None] * inv[None, :]
    cos, sin = jnp.cos(ang), jnp.sin(ang)
    t = pos.shape[0]
    ones = jnp.ones((t, A_DK - ROT_DIM), F32)
    zeros_h = jnp.zeros((t, half), F32)
    zeros_r = jnp.zeros((t, A_DK - ROT_DIM), F32)
    c = jnp.concatenate([cos, cos, ones], axis=1)
    s1 = jnp.concatenate([zeros_h, sin, zeros_r], axis=1)
    s2 = jnp.concatenate([-sin, zeros_h, zeros_r], axis=1)
    rep = lambda a: jnp.concatenate([a] * (LANES // A_DK), axis=1)
    return rep(c), rep(s1), rep(s2)


def _block_ones2(n, blk):
    i = jnp.arange(n) // blk
    e = (i[:, None] == i[None, :]).astype(BF16)
    return jnp.concatenate([e, e], axis=0)


def _split_mod(mod, per_row_reps):
    parts = jnp.split(mod, 6, axis=-1)
    if per_row_reps is None:
        return [m[:, None, :] for m in parts]
    return [jnp.repeat(m, per_row_reps, axis=0)[None] for m in parts]


def kernel(x_prompt, x_sample, c_prompt, c_sample, cache_k, cache_v, state_wkv, state_shift, page_table, norm1_g,
           norm2_g, ada_w, ada_b, w_in, shift_mu, rw_w0, rw_wb, rw_a0, rw_ab, rw_gb, rw_v0, rw_va, rw_vb, rw_kk,
           rw_ka, rw_rk, rw_ln_w, rw_ln_b, at_qnorm, at_knorm, at_lq1, at_lk1, at_lq2, at_lk2, at_subln, br_rwkv,
           br_attn, w_out, mlp_w1, mlp_w2):
    bp, tp, d = x_prompt.shape
    bs, ts, _ = x_sample.shape
    depth = w_in.shape[0]
    rw = rw_w0.shape[1]
    shift_cols = shift_mu.shape[1]
    qk_w = A_HEADS * 2 * A_DK
    v_w = A_HEADS * 2 * A_DK
    n_phys = cache_k.shape[1]
    n_pages = page_table.shape[1]
    past_len = n_pages * PAGE_SIZE
    r_heads = rw // R_HEAD

    mod_all = _ada_mod(jnp.concatenate([c_prompt, c_sample], axis=0), ada_w, ada_b)
    rope_p = _rope_tables(jnp.arange(tp))
    rope_s = tuple(jnp.tile(a, (bs, 1)) for a in _rope_tables(past_len + jnp.arange(ts)))
    e_qk = _block_ones2(qk_w, A_DK)
    e_rw = _block_ones2(rw, R_HEAD)
    ckt = jnp.transpose(cache_k, (0, 1, 3, 4, 5, 2)).reshape(depth * n_phys, qk_w, PAGE_SIZE)
    cv = cache_v.reshape(depth * n_phys, PAGE_SIZE * A_HEADS, v_w // A_HEADS)
    tpad = SUBLANES
    lane = jnp.arange(qk_w)
    q_own = ((lane // LANES)[None, None, :] == jnp.arange(A_HEADS)[:, None, None]) & (
        ((lane // A_DK) % 2)[None, None, :] == jnp.arange(2)[None, :, None])

    xp = x_prompt.reshape(bp * tp, d)
    xs = x_sample.reshape(bs * ts, d)
    rows_s = bs * ts
    outs = {n: [] for n in ("ks", "vs", "sp", "ss", "shp", "shs")}
    vfp = vfs = None
    for l in range(depth):
        lam_init = 0.8 - 0.6 * math.exp(-0.3 * l)
        lam = (jnp.exp(jnp.sum(at_lq1[l] * at_lk1[l]).astype(F32)) - jnp.exp(jnp.sum(at_lq2[l] * at_lk2[l]).astype(F32))
               + lam_init).reshape(1)
        pad_w = jnp.zeros((A_RANK, rw), F32)
        p = dict(
            shift_mu=shift_mu[l][None], rw_w0=rw_w0[l][None], rw_a0=rw_a0[l][None],
            wb_pad=jnp.concatenate([rw_wb[l], pad_w], axis=0).astype(BF16),
            ab_pad=jnp.concatenate([jnp.zeros((W_RANK, rw), F32), rw_ab[l]], axis=0).astype(BF16),
            rw_gb=rw_gb[l].astype(BF16), rw_kk=rw_kk[l][None], rw_ka=rw_ka[l][None])
        if l > 0:
            p.update(rw_v0=rw_v0[l - 1][None], rw_va=rw_va[l - 1].astype(BF16), rw_vb=rw_vb[l - 1].astype(BF16))
        w_in_b = w_in[l].astype(BF16)
        qn = jnp.tile(at_qnorm[l], qk_w // A_DK)[None]
        kn = jnp.tile(at_knorm[l], qk_w // A_DK)[None]
        subln = at_subln[l][None]
        rk = rw_rk[l].reshape(1, rw)
        lnw, lnb = rw_ln_w[l][None], rw_ln_b[l][None]
        brw, bra, wo = br_rwkv[l].astype(BF16), br_attn[l].astype(BF16), w_out[l].astype(BF16)
        w1, w2 = mlp_w1[l].astype(BF16), mlp_w2[l].astype(BF16)
        g1, g2 = norm1_g[l][None], norm2_g[l][None]
        scale_out = 1.0 - lam_init

        sh1, sc1, gt1, sh2, sc2, gt2 = _split_mod(mod_all[l, :bp], None)
        zs, q, kt_all, kb, v_all, vb, gates = _inproj(
            xp, (sc1, sh1), g1, w_in_b, qn, kn, e_qk, rope_p, shift_cols, qk_w, v_w, TM_INPROJ, tp // TM_INPROJ,
            stack=(l, depth, None if l == 0 else (kt_all, v_all)))
        att = _attn_prompt(q.reshape(bp, tp, qk_w), kb.reshape(bp, tp, qk_w), vb.reshape(bp, tp, v_w), lam, subln,
                           scale_out, ATTN_TQ, ATTN_TK).reshape(bp * tp, v_w)
        res = _rwkv(zs.reshape(bp, tp, shift_cols), jnp.zeros((bp, 1, shift_cols), F32), p, e_rw, rk, lnw, lnb,
                    jnp.zeros((bp, r_heads // 2, LANES, LANES), F32), vfp, CHUNK, WKV_SEQS_PER_STEP)
        rwo, zfin = res[:2]
        if l == 0:
            vfp = res[2]
        x1 = _merge(rwo.reshape(bp * tp, rw), att, gates, xp, gt1, brw, bra, wo, TM_MERGE, tp // TM_MERGE)
        xp = _mlp(x1, (sc2, sh2, gt2), g2, w1, w2, TM_MLP, tp // TM_MLP)
        outs["sp"].append(_blockdiag_to_state(zfin))
        outs["shp"].append(zs.reshape(bp, tp, shift_cols)[:, -1])

        sh1, sc1, gt1, sh2, sc2, gt2 = _split_mod(mod_all[l, bp:], ts)
        zs, q, k, kb, v, vb, gates = _inproj(xs, (sc1, sh1), g1, w_in_b, qn, kn, e_qk, rope_s, shift_cols, qk_w, v_w,
                                             rows_s, 1)
        q3 = jnp.pad(q.reshape(bs, ts, qk_w), ((0, 0), (0, tpad - ts), (0, 0)))
        qbig = jnp.where(q_own[None, :, :, None, :], q3[:, None, None], 0).reshape(bs, 2 * A_HEADS * tpad, qk_w)
        pad_new = lambda a: jnp.pad(a.reshape(bs, ts, -1), ((0, 0), (0, PAGE_SIZE - ts), (0, 0)))
        att = _attn_decode(qbig, jnp.swapaxes(pad_new(k), 1, 2), pad_new(v), ckt, cv, page_table + l * n_phys, lam,
                           subln, scale_out)
        att = att[:, :ts].reshape(rows_s, v_w)
        zs3 = zs.reshape(bs, ts, shift_cols)
        res = _rwkv(jnp.pad(zs3, ((0, 0), (0, CHUNK - ts), (0, 0))), state_shift[l][:, None, :], p, e_rw, rk, lnw, lnb,
                    _state_to_blockdiag(state_wkv[l]), vfs, ts, WKV_SEQS_PER_STEP)
        rwo, zfin = res[:2]
        if l == 0:
            vfs = res[2]
        rwo = rwo[:, :ts].reshape(rows_s, rw)
        x1 = _merge(rwo, att, gates, xs, gt1, brw, bra, wo, rows_s, 1)
        xs = _mlp(x1, (sc2, sh2, gt2), g2, w1, w2, rows_s, 1)
        outs["ks"].append(k.reshape(bs, ts, A_HEADS, 2, A_DK))
        outs["vs"].append(v.reshape(bs, ts, A_HEADS, 2 * A_DK))
        outs["ss"].append(_blockdiag_to_state(zfin))
        outs["shs"].append(zs3[:, -1])

    st = lambda n: jnp.stack(outs[n])
    new_k_prompt = jnp.transpose(kt_all.reshape(depth, bp, A_HEADS, 2, A_DK, tp), (0, 1, 5, 2, 3, 4))
    new_v_prompt = v_all.reshape(depth, bp, tp, A_HEADS, 2 * A_DK)
    return (xp.reshape(bp, tp, d), xs.reshape(bs, ts, d), new_k_prompt, new_v_prompt, st("ks"), st("vs"), st("sp"), st("ss"),
            st("shp"), st("shs"))
```

```python
import functools
import math

import jax
import jax.numpy as jnp
from jax import lax
from jax.experimental import pallas as pl
from jax.experimental.pallas import tpu as pltpu

F32, BF16 = jnp.float32, jnp.bfloat16

R_HEAD = 64
GN_EPS = 64e-5
A_HEADS = 4
A_DK = 64
ROT_DIM = A_DK // 4
ROPE_THETA = 500000.0
NORM_EPS = 1e-6
PAGE_SIZE = 128
W_RANK, A_RANK, G_RANK = 64, 64, 128

LANES = 128
SUBLANES = 8
VMEM_LIMIT = 56 * 1024 * 1024

CHUNK = 64
WKV_SEQS_PER_STEP = 4
TM_INPROJ, TM_MERGE, TM_MLP = 512, 512, 512
ATTN_TQ, ATTN_TK = 256, 512
PAGES_PER_STEP = 32

NN = (((1,), (0,)), ((), ()))
NT = (((1,), (1,)), ((), ()))


def _cparams(*sem):
    return pltpu.CompilerParams(dimension_semantics=sem, vmem_limit_bytes=VMEM_LIMIT)


def _dot(a, b, dims=NN):
    return lax.dot_general(a.astype(BF16), b.astype(BF16), dims, preferred_element_type=F32)


def _split2(x):
    hi = x.astype(BF16)
    lo = (x - hi.astype(F32)).astype(BF16)
    return hi, lo


def _mm3(a, b, dims=NN):
    a_hi, a_lo = _split2(a)
    b_hi, b_lo = _split2(b)
    return lax.dot_general(jnp.concatenate([a_hi, a_lo, a_hi], axis=1),
                           jnp.concatenate([b_hi, b_hi, b_lo], axis=dims[0][1][0]), dims,
                           preferred_element_type=F32)


def _seg(x, e2):
    return jnp.dot(jnp.concatenate(_split2(x), axis=1), e2, preferred_element_type=F32)


def _sigmoid(x):
    return 1.0 / (1.0 + jnp.exp(-x))


def _rms_mod(x, g, sc, sh):
    y = x * lax.rsqrt(jnp.mean(x * x, axis=-1, keepdims=True) + NORM_EPS) * g
    return y * (1.0 + sc) + sh


def _ada_kernel(c_ref, w_ref, b_ref, o_ref):
    c = c_ref[...]
    o_ref[0] = _dot(c * _sigmoid(c), w_ref[0]) + b_ref[0]


def _ada_mod(c_all, ada_w, ada_b):
    depth, d, n = ada_w.shape
    rows = c_all.shape[0]
    tn = d
    return pl.pallas_call(
        _ada_kernel,
        grid=(depth, n // tn),
        in_specs=[
            pl.BlockSpec((rows, d), lambda l, j: (0, 0)),
            pl.BlockSpec((1, d, tn), lambda l, j: (l, 0, j)),
            pl.BlockSpec((1, 1, tn), lambda l, j: (l, 0, j)),
        ],
        out_specs=pl.BlockSpec((1, rows, tn), lambda l, j: (l, 0, j)),
        out_shape=jax.ShapeDtypeStruct((depth, rows, n), F32),
        compiler_params=_cparams("parallel", "parallel"),
        name="ada_mod",
    )(c_all, ada_w, ada_b.reshape(depth, 1, n))


def _inproj_kernel(shift_cols, qk_w, v_w, stacked, *refs):
    x_ref, sc_ref, sh_ref, g_ref, w_ref, qn_ref, kn_ref, e_ref, cos_ref, s1_ref, s2_ref = refs[-18:-7]
    zs_ref, q_ref, k_ref, kb_ref, v_ref, vb_ref, gate_ref = refs[-7:]
    hb = _rms_mod(x_ref[...], g_ref[...], sc_ref[0], sh_ref[0]).astype(BF16)
    o = 0
    zs_ref[...] = jnp.dot(hb, w_ref[:, o:o + shift_cols], preferred_element_type=F32)
    o += shift_cols
    reps = qk_w // LANES
    cos = jnp.concatenate([cos_ref[...]] * reps, axis=1)
    s1 = jnp.concatenate([s1_ref[...]] * reps, axis=1)
    s2 = jnp.concatenate([s2_ref[...]] * reps, axis=1)

    def norm_rope(z, gain):
        ms = _seg(z * z, e_ref[...]) * (1.0 / A_DK)
        y = z * lax.rsqrt(ms + NORM_EPS) * gain
        return y * cos + pltpu.roll(y, ROT_DIM // 2, 1) * s1 + pltpu.roll(y, qk_w - ROT_DIM // 2, 1) * s2

    q = norm_rope(jnp.dot(hb, w_ref[:, o:o + qk_w], preferred_element_type=F32), qn_ref[...])
    q_ref[...] = (q * (A_DK ** -0.5)).astype(BF16)
    o += qk_w
    k = norm_rope(jnp.dot(hb, w_ref[:, o:o + qk_w], preferred_element_type=F32), kn_ref[...])
    if stacked:
        k_ref[0, 0] = k.T
    else:
        k_ref[...] = k
    kb_ref[...] = k.astype(BF16)
    o += qk_w
    v = jnp.dot(hb, w_ref[:, o:o + v_w], preferred_element_type=F32)
    if stacked:
        for h in range(v_w // LANES):
            v_ref[0, pl.ds(h, v.shape[0], stride=v_w // LANES), :] = v[:, h * LANES:(h + 1) * LANES]
    else:
        v_ref[...] = v
    vb_ref[...] = v.astype(BF16)
    o += v_w
    gate_ref[...] = _sigmoid(jnp.dot(hb, w_ref[:, o:], preferred_element_type=F32))


def _inproj(x, mods, g, w_bf, qn, kn, e_qk, rope, shift_cols, qk_w, v_w, tm, tps, stack=None):
    rows, d = x.shape
    in_cols = w_bf.shape[1]
    gate_w = in_cols - shift_cols - 2 * qk_w - v_w
    sc, sh = mods
    r_mod = sc.shape[1]
    row = lambda i: (i, 0)
    const = lambda i: (0, 0)
    modmap = lambda i: (i // tps, 0, 0)
    ropemap = lambda i: (i % tps, 0)
    outs = [(shift_cols, F32), (qk_w, BF16), (qk_w, F32), (qk_w, BF16), (v_w, F32), (v_w, BF16), (gate_w, F32)]
    out_specs = [pl.BlockSpec((tm, w), row) for w, _ in outs]
    out_shape = [jax.ShapeDtypeStruct((rows, w), dt) for w, dt in outs]
    extra_in, extra_specs, aliases = [], [], {}
    if stack is not None:
        layer, depth, prev = stack
        heads = v_w // LANES
        seq_len = tm * tps
        out_specs[2] = pl.BlockSpec((1, 1, qk_w, tm), lambda i: (layer, i // tps, 0, i % tps))
        out_shape[2] = jax.ShapeDtypeStruct((depth, rows // seq_len, qk_w, seq_len), F32)
        out_specs[4] = pl.BlockSpec((1, tm * heads, LANES), lambda i: (layer, i, 0))
        out_shape[4] = jax.ShapeDtypeStruct((depth, rows * heads, LANES), F32)
        if prev is not None:
            extra_in = list(prev)
            extra_specs = [pl.BlockSpec(memory_space=pl.ANY)] * 2
            aliases = {0: 2, 1: 4}
    return pl.pallas_call(
        functools.partial(_inproj_kernel, shift_cols, qk_w, v_w, stack is not None),
        grid=(rows // tm,),
        input_output_aliases=aliases,
        in_specs=extra_specs + [
            pl.BlockSpec((tm, d), row),
            pl.BlockSpec((1, r_mod, d), modmap),
            pl.BlockSpec((1, r_mod, d), modmap),
            pl.BlockSpec((1, d), const),
            pl.BlockSpec((d, in_cols), const, pipeline_mode=pl.Buffered(1)),
            pl.BlockSpec((1, qk_w), const),
            pl.BlockSpec((1, qk_w), const),
            pl.BlockSpec((2 * qk_w, qk_w), const),
            pl.BlockSpec((tm, LANES), ropemap),
            pl.BlockSpec((tm, LANES), ropemap),
            pl.BlockSpec((tm, LANES), ropemap),
        ],
        out_specs=out_specs,
        out_shape=out_shape,
        compiler_params=_cparams("parallel"),
        name="inproj",
    )(*extra_in, x, sc, sh, g, w_bf, qn, kn, e_qk, *rope)


def _rwkv_kernel(has_vres, valid, *refs):
    (zs_ref, first_ref, mu_ref, w0_ref, wb_ref, a0_ref, ab_ref, gb_ref, kk_ref, ka_ref, e_ref, rk_ref, lnw_ref,
     lnb_ref, z0_ref) = refs[:15]
    if has_vres:
        v0_ref, va_ref, vb_ref, vf_ref, y_ref, z_ref, carry_ref = refs[15:]
    else:
        y_ref, z_ref, vout_ref, carry_ref = refs[15:]
    C = CHUNK
    P2 = 2 * C
    nb, _, cols = zs_ref.shape
    rw = rk_ref.shape[1]
    chunk = pl.program_id(1)

    @pl.when(chunk == 0)
    def _():
        z_ref[...] = z0_ref[...]
        for s in range(nb):
            carry_ref[s] = jnp.broadcast_to(first_ref[s], (SUBLANES, cols))

    def prep():
        row_id = lax.broadcasted_iota(jnp.int32, (C, 1), 0)
        zs, prevs = [], []
        for s in range(nb):
            z = zs_ref[s]
            prevs.append(jnp.where(row_id == 0, carry_ref[s, SUBLANES - 1:SUBLANES, :], pltpu.roll(z, 1, 0)))
            carry_ref[s] = z[C - SUBLANES:, :]
            zs.append(z)
        z = jnp.concatenate(zs, axis=0)
        xs = z + (jnp.concatenate(prevs, axis=0) - z) * mu_ref[...]
        r = xs[:, 0:rw]
        k = xs[:, rw:2 * rw]
        v = xs[:, 2 * rw:3 * rw]
        zwa = xs[:, 3 * rw:3 * rw + W_RANK + A_RANK]
        zg = xs[:, 3 * rw + W_RANK + A_RANK:]
        u = -(w0_ref[...] + _dot(jnp.tanh(zwa), wb_ref[...]))
        softplus = jnp.maximum(u, 0.0) + jnp.log(1.0 + jnp.exp(-jnp.abs(u)))
        lw = -jnp.exp(-softplus - 0.5)
        a = _sigmoid(a0_ref[...] + _dot(zwa, ab_ref[...]))
        g = _dot(_sigmoid(zg), gb_ref[...])
        if has_vres:
            vg = _sigmoid(v0_ref[...] + _dot(_dot(v, va_ref[...]), vb_ref[...]))
            v = v + (jnp.concatenate([vf_ref[s] for s in range(nb)], axis=0) - v) * vg
        else:
            for s in range(nb):
                vout_ref[s] = v[s * C:(s + 1) * C]
        kk = k * kk_ref[...]
        kk = kk / jnp.maximum(jnp.sqrt(_seg(kk * kk, e_ref[...])), 1e-12)
        k = k * (1.0 + (a - 1.0) * ka_ref[...])
        b = kk * a
        if valid < C:
            keep = (lax.broadcasted_iota(jnp.int32, (nb * C, 1), 0) & (C - 1)) < valid
            lw, k, v, kk, b = (jnp.where(keep, t, 0.0) for t in (lw, k, v, kk, b))
        return r, lw, k, v, kk, b, g

    ri = lax.broadcasted_iota(jnp.int32, (P2, P2), 0)
    ci = lax.broadcasted_iota(jnp.int32, (P2, P2), 1)
    strict = ci < ri
    incl = ci <= ri
    eye = (ci == ri).astype(F32)
    blk = lambda s: (ri >> s) == (ci >> s)
    e_pair = blk(6).astype(BF16)
    e_head = jnp.concatenate([e_pair, e_pair], axis=0)
    ti = lax.broadcasted_iota(jnp.int32, (C, C), 0)
    tj = lax.broadcasted_iota(jnp.int32, (C, C), 1)
    tril1 = (tj <= ti).astype(BF16)
    head_a = lax.broadcasted_iota(jnp.int32, (C, LANES), 1) < R_HEAD

    def stack(x):
        return jnp.concatenate([jnp.where(head_a, x, 0.0), jnp.where(head_a, 0.0, x)], axis=0)

    units = [(s, p) for s in range(nb) for p in range(rw // LANES)]
    pairs = range(len(units))
    sls = [slice(p * LANES, (p + 1) * LANES) for _, p in units]
    mm = _dot
    cat = jnp.concatenate
    tok = prep()
    ld = lambda x: [x[s * C:(s + 1) * C, sl] for (s, _), sl in zip(units, sls)]
    r, lw, k, v, kp, b, gate = (ld(x) for x in tok)

    def split3(x):
        x0 = x.astype(BF16)
        x1 = x - x0.astype(F32)
        x1b = x1.astype(BF16)
        x2b = (x1 - x1b.astype(F32)).astype(BF16)
        return cat([x0, x1b, x2b], axis=0)

    logp_all = jnp.dot(cat([tril1] * 3, axis=1), cat([split3(x) for x in lw], axis=1), preferred_element_type=F32)
    logp = [logp_all[:, p * LANES:(p + 1) * LANES] for p in pairs]
    logpc = [x[C - 1:C, :] for x in logp]
    e_neg = [jnp.exp(-x) for x in logp]
    e_end = [jnp.exp(xc - x) for xc, x in zip(logpc, logp)]
    rt = [stack(r[p] * jnp.exp(logp[p])) for p in pairs]
    kpt = [stack(kp[p] * jnp.exp(logp[p] - lw[p])) for p in pairs]
    v_s = [stack(v[p]) for p in pairs]
    gram = [_dot(cat([kpt[p], rt[p]], axis=0), cat([stack(b[p] * e_neg[p]), stack(k[p] * e_neg[p])], axis=0), NT)
            for p in pairs]
    lb = [jnp.where(strict, g[:P2, :P2], 0.0) for g in gram]
    lk = [jnp.where(strict, g[:P2, P2:], 0.0) for g in gram]
    mb = [jnp.where(incl, g[P2:, :P2], 0.0) for g in gram]
    mk = [jnp.where(incl, g[P2:, P2:], 0.0) for g in gram]
    x = [mm(lk[p], v_s[p]) for p in pairs]
    n1 = [jnp.where(blk(3), -m, 0.0) for m in lb]
    n2 = [mm(m, m) for m in n1]
    t = [mm(eye + a, eye + c) for a, c in zip(n1, n2)]
    n4 = [mm(m, m) for m in n2]
    t = [mm(a, eye + c) for a, c in zip(t, n4)]
    for s in (3, 4, 5):
        off_mask = blk(s + 1) & jnp.logical_not(blk(s))
        w = [mm(jnp.where(off_mask, lb[p], 0.0), t[p]) for p in pairs]
        t = [t[p] - mm(t[p], w[p]) for p in pairs]
    wu = [-mm(t[p], cat([kpt[p], x[p]], axis=1)) for p in pairs]
    lhs = [cat([cat([mb[p], mk[p]], axis=1),
                cat([stack(b[p] * e_end[p]).T, stack(k[p] * e_end[p]).T], axis=1)], axis=0) for p in pairs]
    rhs = [cat([wu[p], cat([jnp.zeros_like(v_s[p]), v_s[p]], axis=1)], axis=0) for p in pairs]
    big = [mm(lhs[p], rhs[p]) for p in pairs]
    qa = [cat([rt[p] + big[p][:P2, :LANES], eye * jnp.exp(logpc[p]) + big[p][P2:, :LANES]], axis=0) for p in pairs]
    qaz = [_mm3(qa[p], z_ref[units[p]]) for p in pairs]
    for p in pairs:
        z_ref[units[p]] = qaz[p][P2:] + big[p][P2:, LANES:]
    ys = [qaz[p][:P2] + big[p][:P2, LANES:] for p in pairs]
    y = cat([x[:C] + x[C:] for x in ys], axis=0)
    rkr = cat([r[p] * k[p] * rk_ref[:, sls[p]] for p in pairs], axis=0)
    sums = _seg(cat([y, rkr], axis=0), e_head)
    n_rows = y.shape[0]
    d = y - sums[:n_rows] * (1.0 / R_HEAD)
    inv_std = lax.rsqrt(_seg(d * d, e_head) * (1.0 / R_HEAD) + GN_EPS)
    for p in pairs:
        seq = units[p][0]
        rows = slice(p * C, (p + 1) * C)
        yn = d[rows] * inv_std[rows] * lnw_ref[:, sls[p]] + lnb_ref[:, sls[p]]
        bonus = sums[n_rows:][rows] * v[p]
        y_ref[seq, :, sls[p]] = ((yn + bonus) * gate[p]).astype(y_ref.dtype)


def _rwkv(zs, first, p, e_rw, rk, lnw, lnb, z0, v_first, valid, nb):
    bsz, t, cols = zs.shape
    rw = rk.shape[1]
    assert bsz % nb == 0 and t % CHUNK == 0 and rw % LANES == 0, (zs.shape, nb)
    n_pairs = rw // LANES
    has_vres = v_first is not None
    seq = lambda bb, c: (bb, c, 0)
    per_b = lambda bb, c: (bb, 0, 0)
    const = lambda bb, c: (0, 0)
    zmap = lambda bb, c: (bb, 0, 0, 0)
    full = lambda a: pl.BlockSpec(a.shape, const)
    params = [p["shift_mu"], p["rw_w0"], p["wb_pad"], p["rw_a0"], p["ab_pad"], p["rw_gb"], p["rw_kk"], p["rw_ka"],
              e_rw, rk, lnw, lnb]
    args = [zs, first] + params + [z0]
    in_specs = ([pl.BlockSpec((nb, CHUNK, cols), seq), pl.BlockSpec((nb, 1, cols), per_b)] + [full(a) for a in params]
                + [pl.BlockSpec((nb, n_pairs, LANES, LANES), zmap)])
    out_specs = [pl.BlockSpec((nb, CHUNK, rw), seq), pl.BlockSpec((nb, n_pairs, LANES, LANES), zmap)]
    out_shape = [jax.ShapeDtypeStruct((bsz, t, rw), BF16), jax.ShapeDtypeStruct((bsz, n_pairs, LANES, LANES), F32)]
    if has_vres:
        extra = [p["rw_v0"], p["rw_va"], p["rw_vb"]]
        args += extra + [v_first]
        in_specs += [full(a) for a in extra] + [pl.BlockSpec((nb, CHUNK, rw), seq)]
    else:
        out_specs.append(pl.BlockSpec((nb, CHUNK, rw), seq))
        out_shape.append(jax.ShapeDtypeStruct((bsz, t, rw), F32))
    return pl.pallas_call(
        functools.partial(_rwkv_kernel, has_vres, valid),
        grid=(bsz // nb, t // CHUNK),
        in_specs=in_specs,
        out_specs=out_specs,
        out_shape=out_shape,
        scratch_shapes=[pltpu.VMEM((nb, SUBLANES, cols), F32)],
        compiler_params=_cparams("parallel", "arbitrary"),
        name="rwkv",
    )(*args)


def _state_to_blockdiag(s):
    bsz, h, n, _ = s.shape
    zt = jnp.swapaxes(s, -1, -2).reshape(bsz, h // 2, 2, n, n)
    zero = jnp.zeros_like(zt[:, :, 0])
    top = jnp.concatenate([zt[:, :, 0], zero], axis=-1)
    bot = jnp.concatenate([zero, zt[:, :, 1]], axis=-1)
    return jnp.concatenate([top, bot], axis=-2)


def _blockdiag_to_state(z):
    n = R_HEAD
    za = z[:, :, :n, :n]
    zb = z[:, :, n:, n:]
    s = jnp.stack([za, zb], axis=2)
    s = s.reshape(z.shape[0], -1, n, n)
    return jnp.swapaxes(s, -1, -2)


def _attn_kernel(scale_out, tk, lam_ref, q_ref, k_ref, v_ref, sg_ref, o_ref, m_ref, l_ref, acc_ref):
    tq = q_ref.shape[1]
    heads = range(q_ref.shape[2] // LANES)
    hsl = [slice(h * LANES, (h + 1) * LANES) for h in heads]
    qi = pl.program_id(1)
    jd = qi // (tk // tq)
    first = lax.broadcasted_iota(jnp.int32, (tq, LANES), 1) < A_DK
    q2 = []
    for h in heads:
        q = q_ref[0, :, hsl[h]]
        zero = jnp.zeros_like(q)
        q2.append(jnp.concatenate([jnp.where(first, q, zero), jnp.where(first, zero, q)], axis=0))
    m_ref[...] = jnp.full(m_ref.shape, -jnp.inf, F32)
    l_ref[...] = jnp.zeros(l_ref.shape, F32)
    acc_ref[...] = jnp.zeros(acc_ref.shape, F32)
    reps = tk // LANES

    def block(start, mask):
        s = [lax.dot_general(q2[h], k_ref[0, pl.ds(start, tk), hsl[h]], NT, preferred_element_type=F32)
             for h in heads]
        if mask is not None:
            s = [jnp.where(mask, x, -jnp.inf) for x in s]
        m_prev = [m_ref[h] for h in heads]
        m_next = [jnp.maximum(m_prev[h], jnp.max(s[h], axis=1, keepdims=True)) for h in heads]
        p = [jnp.exp(s[h] - jnp.concatenate([m_next[h]] * reps, axis=1)) for h in heads]
        alpha = [jnp.exp(m_prev[h] - m_next[h]) for h in heads]
        for h in heads:
            l_ref[h] = alpha[h] * l_ref[h] + jnp.sum(p[h], axis=1, keepdims=True)
            m_ref[h] = m_next[h]
        pv = [jnp.dot(p[h].astype(BF16), v_ref[0, pl.ds(start, tk), hsl[h]], preferred_element_type=F32)
              for h in heads]
        for h in heads:
            acc_ref[h] = alpha[h] * acc_ref[h] + pv[h]

    def body(j, carry):
        block(pl.multiple_of(j * tk, tk), None)
        return carry

    lax.fori_loop(0, jd, body, 0)
    row = lax.broadcasted_iota(jnp.int32, (2 * tq, tk), 0)
    col = lax.broadcasted_iota(jnp.int32, (2 * tq, tk), 1)
    row = jnp.where(row >= tq, row - tq, row) + (qi * tq - jd * tk)
    block(pl.multiple_of(jd * tk, tk), col <= row)
    for h in heads:
        o = acc_ref[h] / l_ref[h]
        att = o[:tq] - lam_ref[0] * o[tq:]
        att = att * lax.rsqrt(jnp.mean(att * att, axis=-1, keepdims=True) + NORM_EPS) * sg_ref[...]
        o_ref[0, :, hsl[h]] = (att * scale_out).astype(o_ref.dtype)


def _attn_prompt(q, k, v, lam, subln, scale_out, tq, tk):
    bsz, t, w = q.shape
    heads = w // LANES
    return pl.pallas_call(
        functools.partial(_attn_kernel, scale_out, tk),
        grid=(bsz, t // tq),
        in_specs=[
            pl.BlockSpec(memory_space=pltpu.SMEM),
            pl.BlockSpec((1, tq, w), lambda b, i: (b, i, 0)),
            pl.BlockSpec((1, t, w), lambda b, i: (b, 0, 0)),
            pl.BlockSpec((1, t, w), lambda b, i: (b, 0, 0)),
            pl.BlockSpec((1, LANES), lambda b, i: (0, 0)),
        ],
        out_specs=pl.BlockSpec((1, tq, w), lambda b, i: (b, i, 0)),
        out_shape=jax.ShapeDtypeStruct((bsz, t, w), BF16),
        scratch_shapes=[pltpu.VMEM((heads, 2 * tq, LANES), F32)] * 3,
        compiler_params=_cparams("parallel", "arbitrary"),
        name="attn_prompt",
    )(lam, q, k, v, subln)


def _decode_kernel(scale_out, n_pages_step, pt_ref, lam_ref, q_ref, kn_ref, vn_ref, *refs):
    k_refs = refs[:n_pages_step]
    v_refs = refs[n_pages_step:2 * n_pages_step]
    sg_ref, o_ref, m_ref, l_ref, acc_ref = refs[2 * n_pages_step:]
    j = pl.program_id(1)
    rows = q_ref.shape[1]
    hrows = rows // A_HEADS
    tpad = hrows // 2

    @pl.when(j == 0)
    def _():
        m_ref[...] = jnp.full(m_ref.shape, -jnp.inf, F32)
        l_ref[...] = jnp.zeros(l_ref.shape, F32)
        acc_ref[...] = jnp.zeros(acc_ref.shape, F32)

    q = q_ref[0]

    def update(s_list, v_fn):
        m_prev = m_ref[...]
        m_cur = s_list[0].max(axis=1, keepdims=True)
        for s in s_list[1:]:
            m_cur = jnp.maximum(m_cur, s.max(axis=1, keepdims=True))
        m_next = jnp.maximum(m_prev, m_cur)
        alpha = jnp.exp(m_prev - m_next)
        l_new = alpha * l_ref[...]
        p_list = []
        for s in s_list:
            p = jnp.exp(s - m_next)
            l_new = l_new + jnp.sum(p, axis=1, keepdims=True)
            p_list.append(p.astype(BF16))
        l_ref[...] = l_new
        m_ref[...] = m_next
        for h in range(A_HEADS):
            hs = slice(h * hrows, (h + 1) * hrows)
            acc = alpha[hs] * acc_ref[hs, :]
            for i, p in enumerate(p_list):
                acc = acc + jnp.dot(p[hs], v_fn(i, h).astype(BF16), preferred_element_type=F32)
            acc_ref[hs, :] = acc

    s_list = [jnp.dot(q, kr[0].astype(BF16), preferred_element_type=F32) for kr in k_refs]
    update(s_list, lambda i, h: v_refs[i][0, pl.ds(h, PAGE_SIZE, stride=A_HEADS), :])

    @pl.when(j == pl.num_programs(1) - 1)
    def _():
        s = jnp.dot(q, kn_ref[0].astype(BF16), preferred_element_type=F32)
        tok = lax.broadcasted_iota(jnp.int32, s.shape, 0) & (tpad - 1)
        key = lax.broadcasted_iota(jnp.int32, s.shape, 1)
        update([jnp.where(key <= tok, s, -jnp.inf)], lambda i, h: vn_ref[0, :, h * LANES:(h + 1) * LANES])
        inv_l = 1.0 / l_ref[...]
        for h in range(A_HEADS):
            r0 = slice(h * hrows, h * hrows + tpad)
            r1 = slice(h * hrows + tpad, (h + 1) * hrows)
            att = acc_ref[r0, :] * inv_l[r0] - lam_ref[0] * (acc_ref[r1, :] * inv_l[r1])
            att = att * lax.rsqrt(jnp.mean(att * att, axis=-1, keepdims=True) + NORM_EPS) * sg_ref[...]
            o_ref[0, :, h * LANES:(h + 1) * LANES] = (att * scale_out).astype(o_ref.dtype)


def _attn_decode(qbig, kt_new, v_new, cache_kt, cache_v, page_ids, lam, subln, scale_out):
    bsz, rows, w = qbig.shape
    n_pages = page_ids.shape[1]
    pps = PAGES_PER_STEP
    tpad = rows // (2 * A_HEADS)

    def page_spec(i):
        return pl.BlockSpec((1, w, PAGE_SIZE), lambda b, j, pt: (pt[b, j * pps + i], 0, 0))

    per_b = lambda b, j, pt: (b, 0, 0)
    grid_spec = pltpu.PrefetchScalarGridSpec(
        num_scalar_prefetch=1,
        grid=(bsz, n_pages // pps),
        in_specs=[
            pl.BlockSpec(memory_space=pltpu.SMEM),
            pl.BlockSpec((1, rows, w), per_b),
            pl.BlockSpec((1, w, PAGE_SIZE), per_b),
            pl.BlockSpec((1, PAGE_SIZE, w), per_b),
        ] + [page_spec(i) for i in range(pps)] + [page_spec(i) for i in range(pps)]
        + [pl.BlockSpec((1, LANES), lambda b, j, pt: (0, 0))],
        out_specs=pl.BlockSpec((1, tpad, w), per_b),
        scratch_shapes=[pltpu.VMEM((rows, LANES), F32)] * 3,
    )
    return pl.pallas_call(
        functools.partial(_decode_kernel, scale_out, pps),
        grid_spec=grid_spec,
        out_shape=jax.ShapeDtypeStruct((bsz, tpad, w), BF16),
        compiler_params=_cparams("parallel", "arbitrary"),
        name="attn_decode",
    )(page_ids, lam, qbig, kt_new, v_new, *([cache_kt] * pps), *([cache_v] * pps), subln)


def _merge_kernel(rw_ref, att_ref, gate_ref, x_ref, gt_ref, brw_ref, bra_ref, wo_ref, o_ref):
    d = x_ref.shape[1]
    m = (gate_ref[:, :d] * jnp.dot(rw_ref[...], brw_ref[...], preferred_element_type=F32)
         + gate_ref[:, d:] * jnp.dot(att_ref[...], bra_ref[...], preferred_element_type=F32))
    o_ref[...] = x_ref[...] + gt_ref[0] * jnp.dot(m.astype(BF16), wo_ref[...], preferred_element_type=F32)


def _merge(rw, att, gates, x, gt, brw, bra, wo, tm, tps):
    rows, d = x.shape
    row = lambda i: (i, 0)
    const = lambda i: (0, 0)
    return pl.pallas_call(
        _merge_kernel,
        grid=(rows // tm,),
        in_specs=[
            pl.BlockSpec((tm, rw.shape[1]), row),
            pl.BlockSpec((tm, att.shape[1]), row),
            pl.BlockSpec((tm, gates.shape[1]), row),
            pl.BlockSpec((tm, d), row),
            pl.BlockSpec((1, gt.shape[1], d), lambda i: (i // tps, 0, 0)),
            pl.BlockSpec(brw.shape, const),
            pl.BlockSpec(bra.shape, const),
            pl.BlockSpec(wo.shape, const),
        ],
        out_specs=pl.BlockSpec((tm, d), row),
        out_shape=jax.ShapeDtypeStruct((rows, d), F32),
        compiler_params=_cparams("parallel"),
        name="merge_out",
    )(rw, att, gates, x, gt, brw, bra, wo)


def _mlp_kernel(fc, x_ref, sc_ref, sh_ref, gt_ref, g_ref, w1_ref, w2_ref, o_ref):
    x = x_ref[...]
    hb = _rms_mod(x, g_ref[...], sc_ref[0], sh_ref[0]).astype(BF16)
    acc = jnp.zeros(x.shape, F32)
    for c in range(w1_ref.shape[1] // fc):
        u = jnp.dot(hb, w1_ref[:, c * fc:(c + 1) * fc], preferred_element_type=F32)
        u = jnp.square(jnp.maximum(u, 0.0)).astype(BF16)
        acc = acc + jnp.dot(u, w2_ref[c * fc:(c + 1) * fc, :], preferred_element_type=F32)
    o_ref[...] = x + gt_ref[0] * acc


def _mlp(x, mods, g, w1, w2, tm, tps):
    rows, d = x.shape
    sc, sh, gt = mods
    r_mod = sc.shape[1]
    row = lambda i: (i, 0)
    const = lambda i: (0, 0)
    modspec = pl.BlockSpec((1, r_mod, d), lambda i: (i // tps, 0, 0))
    return pl.pallas_call(
        functools.partial(_mlp_kernel, 1024),
        grid=(rows // tm,),
        in_specs=[pl.BlockSpec((tm, d), row), modspec, modspec, modspec, pl.BlockSpec((1, d), const),
                  pl.BlockSpec(w1.shape, const, pipeline_mode=pl.Buffered(1)),
                  pl.BlockSpec(w2.shape, const, pipeline_mode=pl.Buffered(1))],
        out_specs=pl.BlockSpec((tm, d), row),
        out_shape=jax.ShapeDtypeStruct((rows, d), F32),
        compiler_params=_cparams("parallel"),
        name="mlp",
    )(x, sc, sh, gt, g, w1, w2)


def _rope_tables(pos):
    half = ROT_DIM // 2
    inv = ROPE_THETA ** (-jnp.arange(half, dtype=F32) / half)
    ang = pos.astype(F32)[:, None] * inv[None, :]
    cos, sin = jnp.cos(ang), jnp.sin(ang)
    t = pos.shape[0]
    ones = jnp.ones((t, A_DK - ROT_DIM), F32)
    zeros_h = jnp.zeros((t, half), F32)
    zeros_r = jnp.zeros((t, A_DK - ROT_DIM), F32)
    c = jnp.concatenate([cos, cos, ones], axis=1)
    s1 = jnp.concatenate([zeros_h, sin, zeros_r], axis=1)
    s2 = jnp.concatenate([-sin, zeros_h, zeros_r], axis=1)
    rep = lambda a: jnp.concatenate([a] * (LANES // A_DK), axis=1)
    return rep(c), rep(s1), rep(s2)


def _block_ones2(n, blk):
    i = jnp.arange(n) // blk
    e = (i[:, None] == i[None, :]).astype(BF16)
    return jnp.concatenate([e, e], axis=0)


def _split_mod(mod, per_row_reps):
    parts = jnp.split(mod, 6, axis=-1)
    if per_row_reps is None:
        return [m[:, None, :] for m in parts]
    return [jnp.repeat(m, per_row_reps, axis=0)[None] for m in parts]


def kernel(x_prompt, x_sample, c_prompt, c_sample, cache_k, cache_v, state_wkv, state_shift, page_table, norm1_g,
           norm2_g, ada_w, ada_b, w_in, shift_mu, rw_w0, rw_wb, rw_a0, rw_ab, rw_gb, rw_v0, rw_va, rw_vb, rw_kk,
           rw_ka, rw_rk, rw_ln_w, rw_ln_b, at_qnorm, at_knorm, at_lq1, at_lk1, at_lq2, at_lk2, at_subln, br_rwkv,
           br_attn, w_out, mlp_w1, mlp_w2):
    bp, tp, d = x_prompt.shape
    bs, ts, _ = x_sample.shape
    depth = w_in.shape[0]
    rw = rw_w0.shape[1]
    shift_cols = shift_mu.shape[1]
    qk_w = A_HEADS * 2 * A_DK
    v_w = A_HEADS * 2 * A_DK
    n_phys = cache_k.shape[1]
    n_pages = page_table.shape[1]
    past_len = n_pages * PAGE_SIZE
    r_heads = rw // R_HEAD

    mod_all = _ada_mod(jnp.concatenate([c_prompt, c_sample], axis=0), ada_w, ada_b)
    rope_p = _rope_tables(jnp.arange(tp))
    rope_s = tuple(jnp.tile(a, (bs, 1)) for a in _rope_tables(past_len + jnp.arange(ts)))
    e_qk = _block_ones2(qk_w, A_DK)
    e_rw = _block_ones2(rw, R_HEAD)
    ckt = jnp.transpose(cache_k, (0, 1, 3, 4, 5, 2)).reshape(depth * n_phys, qk_w, PAGE_SIZE)
    cv = cache_v.reshape(depth * n_phys, PAGE_SIZE * A_HEADS, v_w // A_HEADS)
    tpad = SUBLANES
    lane = jnp.arange(qk_w)
    q_own = ((lane // LANES)[None, None, :] == jnp.arange(A_HEADS)[:, None, None]) & (
        ((lane // A_DK) % 2)[None, None, :] == jnp.arange(2)[None, :, None])

    xp = x_prompt.reshape(bp * tp, d)
    xs = x_sample.reshape(bs * ts, d)
    rows_s = bs * ts
    outs = {n: [] for n in ("ks", "vs", "sp", "ss", "shp", "shs")}
    vfp = vfs = None
    for l in range(depth):
        lam_init = 0.8 - 0.6 * math.exp(-0.3 * l)
        lam = (jnp.exp(jnp.sum(at_lq1[l] * at_lk1[l]).astype(F32)) - jnp.exp(jnp.sum(at_lq2[l] * at_lk2[l]).astype(F32))
               + lam_init).reshape(1)
        pad_w = jnp.zeros((A_RANK, rw), F32)
        p = dict(
            shift_mu=shift_mu[l][None], rw_w0=rw_w0[l][None], rw_a0=rw_a0[l][None],
            wb_pad=jnp.concatenate([rw_wb[l], pad_w], axis=0).astype(BF16),
            ab_pad=jnp.concatenate([jnp.zeros((W_RANK, rw), F32), rw_ab[l]], axis=0).astype(BF16),
            rw_gb=rw_gb[l].astype(BF16), rw_kk=rw_kk[l][None], rw_ka=rw_ka[l][None])
        if l > 0:
            p.update(rw_v0=rw_v0[l - 1][None], rw_va=rw_va[l - 1].astype(BF16), rw_vb=rw_vb[l - 1].astype(BF16))
        w_in_b = w_in[l].astype(BF16)
        qn = jnp.tile(at_qnorm[l], qk_w // A_DK)[None]
        kn = jnp.tile(at_knorm[l], qk_w // A_DK)[None]
        subln = at_subln[l][None]
        rk = rw_rk[l].reshape(1, rw)
        lnw, lnb = rw_ln_w[l][None], rw_ln_b[l][None]
        brw, bra, wo = br_rwkv[l].astype(BF16), br_attn[l].astype(BF16), w_out[l].astype(BF16)
        w1, w2 = mlp_w1[l].astype(BF16), mlp_w2[l].astype(BF16)
        g1, g2 = norm1_g[l][None], norm2_g[l][None]
        scale_out = 1.0 - lam_init

        sh1, sc1, gt1, sh2, sc2, gt2 = _split_mod(mod_all[l, :bp], None)
        zs, q, kt_all, kb, v_all, vb, gates = _inproj(
            xp, (sc1, sh1), g1, w_in_b, qn, kn, e_qk, rope_p, shift_cols, qk_w, v_w, TM_INPROJ, tp // TM_INPROJ,
            stack=(l, depth, None if l == 0 else (kt_all, v_all)))
        att = _attn_prompt(q.reshape(bp, tp, qk_w), kb.reshape(bp, tp, qk_w), vb.reshape(bp, tp, v_w), lam, subln,
                           scale_out, ATTN_TQ, ATTN_TK).reshape(bp * tp, v_w)
        res = _rwkv(zs.reshape(bp, tp, shift_cols), jnp.zeros((bp, 1, shift_cols), F32), p, e_rw, rk, lnw, lnb,
                    jnp.zeros((bp, r_heads // 2, LANES, LANES), F32), vfp, CHUNK, WKV_SEQS_PER_STEP)
        rwo, zfin = res[:2]
        if l == 0:
            vfp = res[2]
        x1 = _merge(rwo.reshape(bp * tp, rw), att, gates, xp, gt1, brw, bra, wo, TM_MERGE, tp // TM_MERGE)
        xp = _mlp(x1, (sc2, sh2, gt2), g2, w1, w2, TM_MLP, tp // TM_MLP)
        outs["sp"].append(_blockdiag_to_state(zfin))
        outs["shp"].append(zs.reshape(bp, tp, shift_cols)[:, -1])

        sh1, sc1, gt1, sh2, sc2, gt2 = _split_mod(mod_all[l, bp:], ts)
        zs, q, k, kb, v, vb, gates = _inproj(xs, (sc1, sh1), g1, w_in_b, qn, kn, e_qk, rope_s, shift_cols, qk_w, v_w,
                                             rows_s, 1)
        q3 = jnp.pad(q.reshape(bs, ts, qk_w), ((0, 0), (0, tpad - ts), (0, 0)))
        qbig = jnp.where(q_own[None, :, :, None, :], q3[:, None, None], 0).reshape(bs, 2 * A_HEADS * tpad, qk_w)
        pad_new = lambda a: jnp.pad(a.reshape(bs, ts, -1), ((0, 0), (0, PAGE_SIZE - ts), (0, 0)))
        att = _attn_decode(qbig, jnp.swapaxes(pad_new(k), 1, 2), pad_new(v), ckt, cv, page_table + l * n_phys, lam,
                           subln, scale_out)
        att = att[:, :ts].reshape(rows_s, v_w)
        zs3 = zs.reshape(bs, ts, shift_cols)
        res = _rwkv(jnp.pad(zs3, ((0, 0), (0, CHUNK - ts), (0, 0))), state_shift[l][:, None, :], p, e_rw, rk, lnw, lnb,
                    _state_to_blockdiag(state_wkv[l]), vfs, ts, WKV_SEQS_PER_STEP)
        rwo, zfin = res[:2]
        if l == 0:
            vfs = res[2]
        rwo = rwo[:, :ts].reshape(rows_s, rw)
        x1 = _merge(rwo, att, gates, xs, gt1, brw, bra, wo, rows_s, 1)
        xs = _mlp(x1, (sc2, sh2, gt2), g2, w1, w2, rows_s, 1)
        outs["ks"].append(k.reshape(bs, ts, A_HEADS, 2, A_DK))
        outs["vs"].append(v.reshape(bs, ts, A_HEADS, 2 * A_DK))
        outs["ss"].append(_blockdiag_to_state(zfin))
        outs["shs"].append(zs3[:, -1])

    st = lambda n: jnp.stack(outs[n])
    new_k_prompt = jnp.transpose(kt_all.reshape(depth, bp, A_HEADS, 2, A_DK, tp), (0, 1, 5, 2, 3, 4))
    new_v_prompt = v_all.reshape(depth, bp, tp, A_HEADS, 2 * A_DK)
    return (xp.reshape(bp, tp, d), xs.reshape(bs, ts, d), new_k_prompt, new_v_prompt, st("ks"), st("vs"), st("sp"), st("ss"),
            st("shp"), st("shs"))
```

```python
import functools
import math

import jax
import jax.numpy as jnp
from jax import lax
from jax.experimental import pallas as pl
from jax.experimental.pallas import tpu as pltpu

F32, BF16 = jnp.float32, jnp.bfloat16

R_HEAD = 64
GN_EPS = 64e-5
A_HEADS = 4
A_DK = 64
ROT_DIM = A_DK // 4
ROPE_THETA = 500000.0
NORM_EPS = 1e-6
PAGE_SIZE = 128
W_RANK, A_RANK, G_RANK = 64, 64, 128

LANES = 128
SUBLANES = 8
VMEM_LIMIT = 56 * 1024 * 1024

CHUNK = 64
WKV_SEQS_PER_STEP = 4
TM_INPROJ, TM_MERGE, TM_MLP = 512, 512, 512
ATTN_TQ, ATTN_TK = 256, 512
PAGES_PER_STEP = 32

NN = (((1,), (0,)), ((), ()))
NT = (((1,), (1,)), ((), ()))


def _cparams(*sem):
    return pltpu.CompilerParams(dimension_semantics=sem, vmem_limit_bytes=VMEM_LIMIT)


def _dot(a, b, dims=NN):
    return lax.dot_general(a.astype(BF16), b.astype(BF16), dims, preferred_element_type=F32)


def _split2(x):
    hi = x.astype(BF16)
    lo = (x - hi.astype(F32)).astype(BF16)
    return hi, lo


def _mm3(a, b, dims=NN):
    a_hi, a_lo = _split2(a)
    b_hi, b_lo = _split2(b)
    return lax.dot_general(jnp.concatenate([a_hi, a_lo, a_hi], axis=1),
                           jnp.concatenate([b_hi, b_hi, b_lo], axis=dims[0][1][0]), dims,
                           preferred_element_type=F32)


def _seg(x, e2):
    return jnp.dot(jnp.concatenate(_split2(x), axis=1), e2, preferred_element_type=F32)


def _sigmoid(x):
    return 1.0 / (1.0 + jnp.exp(-x))


def _rms_mod(x, g, sc, sh):
    y = x * lax.rsqrt(jnp.mean(x * x, axis=-1, keepdims=True) + NORM_EPS) * g
    return y * (1.0 + sc) + sh


def _ada_kernel(c_ref, w_ref, b_ref, o_ref):
    c = c_ref[...]
    o_ref[0] = _dot(c * _sigmoid(c), w_ref[0]) + b_ref[0]


def _ada_mod(c_all, ada_w, ada_b):
    depth, d, n = ada_w.shape
    rows = c_all.shape[0]
    tn = d
    return pl.pallas_call(
        _ada_kernel,
        grid=(depth, n // tn),
        in_specs=[
            pl.BlockSpec((rows, d), lambda l, j: (0, 0)),
            pl.BlockSpec((1, d, tn), lambda l, j: (l, 0, j)),
            pl.BlockSpec((1, 1, tn), lambda l, j: (l, 0, j)),
        ],
        out_specs=pl.BlockSpec((1, rows, tn), lambda l, j: (l, 0, j)),
        out_shape=jax.ShapeDtypeStruct((depth, rows, n), F32),
        compiler_params=_cparams("parallel", "parallel"),
        name="ada_mod",
    )(c_all, ada_w, ada_b.reshape(depth, 1, n))


def _inproj_kernel(shift_cols, qk_w, v_w, stacked, *refs):
    x_ref, sc_ref, sh_ref, g_ref, w_ref, qn_ref, kn_ref, e_ref, cos_ref, s1_ref, s2_ref = refs[-18:-7]
    zs_ref, q_ref, k_ref, kb_ref, v_ref, vb_ref, gate_ref = refs[-7:]
    hb = _rms_mod(x_ref[...], g_ref[...], sc_ref[0], sh_ref[0]).astype(BF16)
    o = 0
    zs_ref[...] = jnp.dot(hb, w_ref[:, o:o + shift_cols], preferred_element_type=F32)
    o += shift_cols
    reps = qk_w // LANES
    cos = jnp.concatenate([cos_ref[...]] * reps, axis=1)
    s1 = jnp.concatenate([s1_ref[...]] * reps, axis=1)
    s2 = jnp.concatenate([s2_ref[...]] * reps, axis=1)

    def norm_rope(z, gain):
        ms = _seg(z * z, e_ref[...]) * (1.0 / A_DK)
        y = z * lax.rsqrt(ms + NORM_EPS) * gain
        return y * cos + pltpu.roll(y, ROT_DIM // 2, 1) * s1 + pltpu.roll(y, qk_w - ROT_DIM // 2, 1) * s2

    q = norm_rope(jnp.dot(hb, w_ref[:, o:o + qk_w], preferred_element_type=F32), qn_ref[...])
    q_ref[...] = (q * (A_DK ** -0.5)).astype(BF16)
    o += qk_w
    k = norm_rope(jnp.dot(hb, w_ref[:, o:o + qk_w], preferred_element_type=F32), kn_ref[...])
    if stacked:
        k_ref[0, 0] = k.T
    else:
        k_ref[...] = k
    kb_ref[...] = k.astype(BF16)
    o += qk_w
    v = jnp.dot(hb, w_ref[:, o:o + v_w], preferred_element_type=F32)
    if stacked:
        for h in range(v_w // LANES):
            v_ref[0, pl.ds(h, v.shape[0], stride=v_w // LANES), :] = v[:, h * LANES:(h + 1) * LANES]
    else:
        v_ref[...] = v
    vb_ref[...] = v.astype(BF16)
    o += v_w
    gate_ref[...] = _sigmoid(jnp.dot(hb, w_ref[:, o:], preferred_element_type=F32))


def _inproj(x, mods, g, w_bf, qn, kn, e_qk, rope, shift_cols, qk_w, v_w, tm, tps, stack=None):
    rows, d = x.shape
    in_cols = w_bf.shape[1]
    gate_w = in_cols - shift_cols - 2 * qk_w - v_w
    sc, sh = mods
    r_mod = sc.shape[1]
    row = lambda i: (i, 0)
    const = lambda i: (0, 0)
    modmap = lambda i: (i // tps, 0, 0)
    ropemap = lambda i: (i % tps, 0)
    outs = [(shift_cols, F32), (qk_w, BF16), (qk_w, F32), (qk_w, BF16), (v_w, F32), (v_w, BF16), (gate_w, F32)]
    out_specs = [pl.BlockSpec((tm, w), row) for w, _ in outs]
    out_shape = [jax.ShapeDtypeStruct((rows, w), dt) for w, dt in outs]
    extra_in, extra_specs, aliases = [], [], {}
    if stack is not None:
        layer, depth, prev = stack
        heads = v_w // LANES
        seq_len = tm * tps
        out_specs[2] = pl.BlockSpec((1, 1, qk_w, tm), lambda i: (layer, i // tps, 0, i % tps))
        out_shape[2] = jax.ShapeDtypeStruct((depth, rows // seq_len, qk_w, seq_len), F32)
        out_specs[4] = pl.BlockSpec((1, tm * heads, LANES), lambda i: (layer, i, 0))
        out_shape[4] = jax.ShapeDtypeStruct((depth, rows * heads, LANES), F32)
        if prev is not None:
            extra_in = list(prev)
            extra_specs = [pl.BlockSpec(memory_space=pl.ANY)] * 2
            aliases = {0: 2, 1: 4}
    return pl.pallas_call(
        functools.partial(_inproj_kernel, shift_cols, qk_w, v_w, stack is not None),
        grid=(rows // tm,),
        input_output_aliases=aliases,
        in_specs=extra_specs + [
            pl.BlockSpec((tm, d), row),
            pl.BlockSpec((1, r_mod, d), modmap),
            pl.BlockSpec((1, r_mod, d), modmap),
            pl.BlockSpec((1, d), const),
            pl.BlockSpec((d, in_cols), const, pipeline_mode=pl.Buffered(1)),
            pl.BlockSpec((1, qk_w), const),
            pl.BlockSpec((1, qk_w), const),
            pl.BlockSpec((2 * qk_w, qk_w), const),
            pl.BlockSpec((tm, LANES), ropemap),
            pl.BlockSpec((tm, LANES), ropemap),
            pl.BlockSpec((tm, LANES), ropemap),
        ],
        out_specs=out_specs,
        out_shape=out_shape,
        compiler_params=_cparams("parallel"),
        name="inproj",
    )(*extra_in, x, sc, sh, g, w_bf, qn, kn, e_qk, *rope)


def _rwkv_kernel(has_vres, valid, *refs):
    (zs_ref, first_ref, mu_ref, w0_ref, wb_ref, a0_ref, ab_ref, gb_ref, kk_ref, ka_ref, e_ref, rk_ref, lnw_ref,
     lnb_ref, z0_ref) = refs[:15]
    if has_vres:
        v0_ref, va_ref, vb_ref, vf_ref, y_ref, z_ref, carry_ref = refs[15:]
    else:
        y_ref, z_ref, vout_ref, carry_ref = refs[15:]
    C = CHUNK
    P2 = 2 * C
    nb, _, cols = zs_ref.shape
    rw = rk_ref.shape[1]
    chunk = pl.program_id(1)

    @pl.when(chunk == 0)
    def _():
        z_ref[...] = z0_ref[...]
        for s in range(nb):
            carry_ref[s] = jnp.broadcast_to(first_ref[s], (SUBLANES, cols))

    def prep():
        row_id = lax.broadcasted_iota(jnp.int32, (C, 1), 0)
        zs, prevs = [], []
        for s in range(nb):
            z = zs_ref[s]
            prevs.append(jnp.where(row_id == 0, carry_ref[s, SUBLANES - 1:SUBLANES, :], pltpu.roll(z, 1, 0)))
            carry_ref[s] = z[C - SUBLANES:, :]
            zs.append(z)
        z = jnp.concatenate(zs, axis=0)
        xs = z + (jnp.concatenate(prevs, axis=0) - z) * mu_ref[...]
        r = xs[:, 0:rw]
        k = xs[:, rw:2 * rw]
        v = xs[:, 2 * rw:3 * rw]
        zwa = xs[:, 3 * rw:3 * rw + W_RANK + A_RANK]
        zg = xs[:, 3 * rw + W_RANK + A_RANK:]
        u = -(w0_ref[...] + _dot(jnp.tanh(zwa), wb_ref[...]))
        softplus = jnp.maximum(u, 0.0) + jnp.log(1.0 + jnp.exp(-jnp.abs(u)))
        lw = -jnp.exp(-softplus - 0.5)
        a = _sigmoid(a0_ref[...] + _dot(zwa, ab_ref[...]))
        g = _dot(_sigmoid(zg), gb_ref[...])
        if has_vres:
            vg = _sigmoid(v0_ref[...] + _dot(_dot(v, va_ref[...]), vb_ref[...]))
            v = v + (jnp.concatenate([vf_ref[s] for s in range(nb)], axis=0) - v) * vg
        else:
            for s in range(nb):
                vout_ref[s] = v[s * C:(s + 1) * C]
        kk = k * kk_ref[...]
        kk = kk / jnp.maximum(jnp.sqrt(_seg(kk * kk, e_ref[...])), 1e-12)
        k = k * (1.0 + (a - 1.0) * ka_ref[...])
        b = kk * a
        if valid < C:
            keep = (lax.broadcasted_iota(jnp.int32, (nb * C, 1), 0) & (C - 1)) < valid
            lw, k, v, kk, b = (jnp.where(keep, t, 0.0) for t in (lw, k, v, kk, b))
        return r, lw, k, v, kk, b, g

    ri = lax.broadcasted_iota(jnp.int32, (P2, P2), 0)
    ci = lax.broadcasted_iota(jnp.int32, (P2, P2), 1)
    strict = ci < ri
    incl = ci <= ri
    eye = (ci == ri).astype(F32)
    blk = lambda s: (ri >> s) == (ci >> s)
    e_pair = blk(6).astype(BF16)
    e_head = jnp.concatenate([e_pair, e_pair], axis=0)
    ti = lax.broadcasted_iota(jnp.int32, (C, C), 0)
    tj = lax.broadcasted_iota(jnp.int32, (C, C), 1)
    tril1 = (tj <= ti).astype(BF16)
    head_a = lax.broadcasted_iota(jnp.int32, (C, LANES), 1) < R_HEAD

    def stack(x):
        return jnp.concatenate([jnp.where(head_a, x, 0.0), jnp.where(head_a, 0.0, x)], axis=0)

    units = [(s, p) for s in range(nb) for p in range(rw // LANES)]
    pairs = range(len(units))
    sls = [slice(p * LANES, (p + 1) * LANES) for _, p in units]
    mm = _dot
    cat = jnp.concatenate
    tok = prep()
    ld = lambda x: [x[s * C:(s + 1) * C, sl] for (s, _), sl in zip(units, sls)]
    r, lw, k, v, kp, b, gate = (ld(x) for x in tok)

    def split3(x):
        x0 = x.astype(BF16)
        x1 = x - x0.astype(F32)
        x1b = x1.astype(BF16)
        x2b = (x1 - x1b.astype(F32)).astype(BF16)
        return cat([x0, x1b, x2b], axis=0)

    logp_all = jnp.dot(cat([tril1] * 3, axis=1), cat([split3(x) for x in lw], axis=1), preferred_element_type=F32)
    logp = [logp_all[:, p * LANES:(p + 1) * LANES] for p in pairs]
    logpc = [x[C - 1:C, :] for x in logp]
    e_neg = [jnp.exp(-x) for x in logp]
    e_end = [jnp.exp(xc - x) for xc, x in zip(logpc, logp)]
    rt = [stack(r[p] * jnp.exp(logp[p])) for p in pairs]
    kpt = [stack(kp[p] * jnp.exp(logp[p] - lw[p])) for p in pairs]
    v_s = [stack(v[p]) for p in pairs]
    gram = [_dot(cat([kpt[p], rt[p]], axis=0), cat([stack(b[p] * e_neg[p]), stack(k[p] * e_neg[p])], axis=0), NT)
            for p in pairs]
    lb = [jnp.where(strict, g[:P2, :P2], 0.0) for g in gram]
    lk = [jnp.where(strict, g[:P2, P2:], 0.0) for g in gram]
    mb = [jnp.where(incl, g[P2:, :P2], 0.0) for g in gram]
    mk = [jnp.where(incl, g[P2:, P2:], 0.0) for g in gram]
    x = [mm(lk[p], v_s[p]) for p in pairs]
    n1 = [jnp.where(blk(3), -m, 0.0) for m in lb]
    n2 = [mm(m, m) for m in n1]
    t = [mm(eye + a, eye + c) for a, c in zip(n1, n2)]
    n4 = [mm(m, m) for m in n2]
    t = [mm(a, eye + c) for a, c in zip(t, n4)]
    for s in (3, 4, 5):
        off_mask = blk(s + 1) & jnp.logical_not(blk(s))
        w = [mm(jnp.where(off_mask, lb[p], 0.0), t[p]) for p in pairs]
        t = [t[p] - mm(t[p], w[p]) for p in pairs]
    wu = [-mm(t[p], cat([kpt[p], x[p]], axis=1)) for p in pairs]
    lhs = [cat([cat([mb[p], mk[p]], axis=1),
                cat([stack(b[p] * e_end[p]).T, stack(k[p] * e_end[p]).T], axis=1)], axis=0) for p in pairs]
    rhs = [cat([wu[p], cat([jnp.zeros_like(v_s[p]), v_s[p]], axis=1)], axis=0) for p in pairs]
    big = [mm(lhs[p], rhs[p]) for p in pairs]
    qa = [cat([rt[p] + big[p][:P2, :LANES], eye * jnp.exp(logpc[p]) + big[p][P2:, :LANES]], axis=0) for p in pairs]
    qaz = [_mm3(qa[p], z_ref[units[p]]) for p in pairs]
    for p in pairs:
        z_ref[units[p]] = qaz[p][P2:] + big[p][P2:, LANES:]
    ys = [qaz[p][:P2] + big[p][:P2, LANES:] for p in pairs]
    y = cat([x[:C] + x[C:] for x in ys], axis=0)
    rkr = cat([r[p] * k[p] * rk_ref[:, sls[p]] for p in pairs], axis=0)
    sums = _seg(cat([y, rkr], axis=0), e_head)
    n_rows = y.shape[0]
    d = y - sums[:n_rows] * (1.0 / R_HEAD)
    inv_std = lax.rsqrt(_seg(d * d, e_head) * (1.0 / R_HEAD) + GN_EPS)
    for p in pairs:
        seq = units[p][0]
        rows = slice(p * C, (p + 1) * C)
        yn = d[rows] * inv_std[rows] * lnw_ref[:, sls[p]] + lnb_ref[:, sls[p]]
        bonus = sums[n_rows:][rows] * v[p]
        y_ref[seq, :, sls[p]] = ((yn + bonus) * gate[p]).astype(y_ref.dtype)


def _rwkv(zs, first, p, e_rw, rk, lnw, lnb, z0, v_first, valid, nb):
    bsz, t, cols = zs.shape
    rw = rk.shape[1]
    assert bsz % nb == 0 and t % CHUNK == 0 and rw % LANES == 0, (zs.shape, nb)
    n_pairs = rw // LANES
    has_vres = v_first is not None
    seq = lambda bb, c: (bb, c, 0)
    per_b = lambda bb, c: (bb, 0, 0)
    const = lambda bb, c: (0, 0)
    zmap = lambda bb, c: (bb, 0, 0, 0)
    full = lambda a: pl.BlockSpec(a.shape, const)
    params = [p["shift_mu"], p["rw_w0"], p["wb_pad"], p["rw_a0"], p["ab_pad"], p["rw_gb"], p["rw_kk"], p["rw_ka"],
              e_rw, rk, lnw, lnb]
    args = [zs, first] + params + [z0]
    in_specs = ([pl.BlockSpec((nb, CHUNK, cols), seq), pl.BlockSpec((nb, 1, cols), per_b)] + [full(a) for a in params]
                + [pl.BlockSpec((nb, n_pairs, LANES, LANES), zmap)])
    out_specs = [pl.BlockSpec((nb, CHUNK, rw), seq), pl.BlockSpec((nb, n_pairs, LANES, LANES), zmap)]
    out_shape = [jax.ShapeDtypeStruct((bsz, t, rw), BF16), jax.ShapeDtypeStruct((bsz, n_pairs, LANES, LANES), F32)]
    if has_vres:
        extra = [p["rw_v0"], p["rw_va"], p["rw_vb"]]
        args += extra + [v_first]
        in_specs += [full(a) for a in extra] + [pl.BlockSpec((nb, CHUNK, rw), seq)]
    else:
        out_specs.append(pl.BlockSpec((nb, CHUNK, rw), seq))
        out_shape.append(jax.ShapeDtypeStruct((bsz, t, rw), F32))
    return pl.pallas_call(
        functools.partial(_rwkv_kernel, has_vres, valid),
        grid=(bsz // nb, t // CHUNK),
        in_specs=in_specs,
        out_specs=out_specs,
        out_shape=out_shape,
        scratch_shapes=[pltpu.VMEM((nb, SUBLANES, cols), F32)],
        compiler_params=_cparams("parallel", "arbitrary"),
        name="rwkv",
    )(*args)


def _state_to_blockdiag(s):
    bsz, h, n, _ = s.shape
    zt = jnp.swapaxes(s, -1, -2).reshape(bsz, h // 2, 2, n, n)
    zero = jnp.zeros_like(zt[:, :, 0])
    top = jnp.concatenate([zt[:, :, 0], zero], axis=-1)
    bot = jnp.concatenate([zero, zt[:, :, 1]], axis=-1)
    return jnp.concatenate([top, bot], axis=-2)


def _blockdiag_to_state(z):
    n = R_HEAD
    za = z[:, :, :n, :n]
    zb = z[:, :, n:, n:]
    s = jnp.stack([za, zb], axis=2)
    s = s.reshape(z.shape[0], -1, n, n)
    return jnp.swapaxes(s, -1, -2)


def _attn_kernel(scale_out, tk, *refs):
    _attn_body(scale_out, tk, pl.program_id(1), *refs)


def _attn_body(scale_out, tk, qi, lam_ref, q_ref, k_ref, v_ref, sg_ref, o_ref, m_ref, l_ref, acc_ref):
    tq = q_ref.shape[1]
    heads = range(q_ref.shape[2] // LANES)
    hsl = [slice(h * LANES, (h + 1) * LANES) for h in heads]
    jd = qi // (tk // tq)
    first = lax.broadcasted_iota(jnp.int32, (tq, LANES), 1) < A_DK
    q2 = []
    for h in heads:
        q = q_ref[0, :, hsl[h]]
        zero = jnp.zeros_like(q)
        q2.append(jnp.concatenate([jnp.where(first, q, zero), jnp.where(first, zero, q)], axis=0))
    m_ref[...] = jnp.full(m_ref.shape, -jnp.inf, F32)
    l_ref[...] = jnp.zeros(l_ref.shape, F32)
    acc_ref[...] = jnp.zeros(acc_ref.shape, F32)
    reps = tk // LANES

    def block(start, mask):
        s = [lax.dot_general(q2[h], k_ref[0, pl.ds(start, tk), hsl[h]], NT, preferred_element_type=F32)
             for h in heads]
        if mask is not None:
            s = [jnp.where(mask, x, -jnp.inf) for x in s]
        m_prev = [m_ref[h] for h in heads]
        m_next = [jnp.maximum(m_prev[h], jnp.max(s[h], axis=1, keepdims=True)) for h in heads]
        p = [jnp.exp(s[h] - jnp.concatenate([m_next[h]] * reps, axis=1)) for h in heads]
        alpha = [jnp.exp(m_prev[h] - m_next[h]) for h in heads]
        for h in heads:
            l_ref[h] = alpha[h] * l_ref[h] + jnp.sum(p[h], axis=1, keepdims=True)
            m_ref[h] = m_next[h]
        pv = [jnp.dot(p[h].astype(BF16), v_ref[0, pl.ds(start, tk), hsl[h]], preferred_element_type=F32)
              for h in heads]
        for h in heads:
            acc_ref[h] = alpha[h] * acc_ref[h] + pv[h]

    def body(j, carry):
        block(pl.multiple_of(j * tk, tk), None)
        return carry

    lax.fori_loop(0, jd, body, 0)
    row = lax.broadcasted_iota(jnp.int32, (2 * tq, tk), 0)
    col = lax.broadcasted_iota(jnp.int32, (2 * tq, tk), 1)
    row = jnp.where(row >= tq, row - tq, row) + (qi * tq - jd * tk)
    block(pl.multiple_of(jd * tk, tk), col <= row)
    for h in heads:
        o = acc_ref[h] / l_ref[h]
        att = o[:tq] - lam_ref[0] * o[tq:]
        att = att * lax.rsqrt(jnp.mean(att * att, axis=-1, keepdims=True) + NORM_EPS) * sg_ref[...]
        o_ref[0, :, hsl[h]] = (att * scale_out).astype(o_ref.dtype)


def _attn_prompt(q, k, v, lam, subln, scale_out, tq, tk):
    bsz, t, w = q.shape
    heads = w // LANES
    return pl.pallas_call(
        functools.partial(_attn_kernel, scale_out, tk),
        grid=(bsz, t // tq),
        in_specs=[
            pl.BlockSpec(memory_space=pltpu.SMEM),
            pl.BlockSpec((1, tq, w), lambda b, i: (b, i, 0)),
            pl.BlockSpec((1, t, w), lambda b, i: (b, 0, 0)),
            pl.BlockSpec((1, t, w), lambda b, i: (b, 0, 0)),
            pl.BlockSpec((1, LANES), lambda b, i: (0, 0)),
        ],
        out_specs=pl.BlockSpec((1, tq, w), lambda b, i: (b, i, 0)),
        out_shape=jax.ShapeDtypeStruct((bsz, t, w), BF16),
        scratch_shapes=[pltpu.VMEM((heads, 2 * tq, LANES), F32)] * 3,
        compiler_params=_cparams("parallel", "arbitrary"),
        name="attn_prompt",
    )(lam, q, k, v, subln)


def _decode_kernel(scale_out, n_pages_step, pt_ref, *refs):
    _decode_body(scale_out, n_pages_step, pl.program_id(1), pl.num_programs(1), *refs)


def _decode_body(scale_out, n_pages_step, j, n_steps, lam_ref, q_ref, kn_ref, vn_ref, *refs):
    k_refs = refs[:n_pages_step]
    v_refs = refs[n_pages_step:2 * n_pages_step]
    sg_ref, o_ref, m_ref, l_ref, acc_ref = refs[2 * n_pages_step:]
    rows = q_ref.shape[1]
    hrows = rows // A_HEADS
    tpad = hrows // 2

    @pl.when(j == 0)
    def _():
        m_ref[...] = jnp.full(m_ref.shape, -jnp.inf, F32)
        l_ref[...] = jnp.zeros(l_ref.shape, F32)
        acc_ref[...] = jnp.zeros(acc_ref.shape, F32)

    q = q_ref[0]

    def update(s_list, v_fn):
        m_prev = m_ref[...]
        m_cur = s_list[0].max(axis=1, keepdims=True)
        for s in s_list[1:]:
            m_cur = jnp.maximum(m_cur, s.max(axis=1, keepdims=True))
        m_next = jnp.maximum(m_prev, m_cur)
        alpha = jnp.exp(m_prev - m_next)
        l_new = alpha * l_ref[...]
        p_list = []
        for s in s_list:
            p = jnp.exp(s - m_next)
            l_new = l_new + jnp.sum(p, axis=1, keepdims=True)
            p_list.append(p.astype(BF16))
        l_ref[...] = l_new
        m_ref[...] = m_next
        for h in range(A_HEADS):
            hs = slice(h * hrows, (h + 1) * hrows)
            acc = alpha[hs] * acc_ref[hs, :]
            for i, p in enumerate(p_list):
                acc = acc + jnp.dot(p[hs], v_fn(i, h).astype(BF16), preferred_element_type=F32)
            acc_ref[hs, :] = acc

    s_list = [jnp.dot(q, kr[0].astype(BF16), preferred_element_type=F32) for kr in k_refs]
    update(s_list, lambda i, h: v_refs[i][0, pl.ds(h, PAGE_SIZE, stride=A_HEADS), :])

    @pl.when(j == n_steps - 1)
    def _():
        s = jnp.dot(q, kn_ref[0].astype(BF16), preferred_element_type=F32)
        tok = lax.broadcasted_iota(jnp.int32, s.shape, 0) & (tpad - 1)
        key = lax.broadcasted_iota(jnp.int32, s.shape, 1)
        update([jnp.where(key <= tok, s, -jnp.inf)], lambda i, h: vn_ref[0, :, h * LANES:(h + 1) * LANES])
        inv_l = 1.0 / l_ref[...]
        for h in range(A_HEADS):
            r0 = slice(h * hrows, h * hrows + tpad)
            r1 = slice(h * hrows + tpad, (h + 1) * hrows)
            att = acc_ref[r0, :] * inv_l[r0] - lam_ref[0] * (acc_ref[r1, :] * inv_l[r1])
            att = att * lax.rsqrt(jnp.mean(att * att, axis=-1, keepdims=True) + NORM_EPS) * sg_ref[...]
            o_ref[0, :, h * LANES:(h + 1) * LANES] = (att * scale_out).astype(o_ref.dtype)


def _attn_decode(qbig, kt_new, v_new, cache_kt, cache_v, page_ids, lam, subln, scale_out):
    bsz, rows, w = qbig.shape
    n_pages = page_ids.shape[1]
    pps = PAGES_PER_STEP
    tpad = rows // (2 * A_HEADS)

    def page_spec(i):
        return pl.BlockSpec((1, w, PAGE_SIZE), lambda b, j, pt: (pt[b, j * pps + i], 0, 0))

    per_b = lambda b, j, pt: (b, 0, 0)
    grid_spec = pltpu.PrefetchScalarGridSpec(
        num_scalar_prefetch=1,
        grid=(bsz, n_pages // pps),
        in_specs=[
            pl.BlockSpec(memory_space=pltpu.SMEM),
            pl.BlockSpec((1, rows, w), per_b),
            pl.BlockSpec((1, w, PAGE_SIZE), per_b),
            pl.BlockSpec((1, PAGE_SIZE, w), per_b),
        ] + [page_spec(i) for i in range(pps)] + [page_spec(i) for i in range(pps)]
        + [pl.BlockSpec((1, LANES), lambda b, j, pt: (0, 0))],
        out_specs=pl.BlockSpec((1, tpad, w), per_b),
        scratch_shapes=[pltpu.VMEM((rows, LANES), F32)] * 3,
    )
    return pl.pallas_call(
        functools.partial(_decode_kernel, scale_out, pps),
        grid_spec=grid_spec,
        out_shape=jax.ShapeDtypeStruct((bsz, tpad, w), BF16),
        compiler_params=_cparams("parallel", "arbitrary"),
        name="attn_decode",
    )(page_ids, lam, qbig, kt_new, v_new, *([cache_kt] * pps), *([cache_v] * pps), subln)


def _attn_both_kernel(scale_out, tk, pps, steps_per_seq, pt_ref, lam_ref, q_ref, k_ref, v_ref, sg_ref, qd_ref, kn_ref,
                      vn_ref, *refs):
    page_refs = refs[:2 * pps]
    o_ref, od_ref, m_ref, l_ref, acc_ref, md_ref, ld_ref, accd_ref = refs[2 * pps:]
    qi = pl.program_id(1)
    _attn_body(scale_out, tk, qi, lam_ref, q_ref, k_ref, v_ref, sg_ref, o_ref, m_ref, l_ref, acc_ref)
    step = pl.program_id(0) * pl.num_programs(1) + qi
    _decode_body(scale_out, pps, step % steps_per_seq, steps_per_seq, lam_ref, qd_ref, kn_ref, vn_ref, *page_refs,
                 sg_ref, od_ref, md_ref, ld_ref, accd_ref)


def _attention(q, k, v, qbig, kt_new, v_new, cache_kt, cache_v, page_ids, lam, subln, scale_out):
    bsz, t, w = q.shape
    bs, rows, _ = qbig.shape
    tq, tk, pps = ATTN_TQ, ATTN_TK, PAGES_PER_STEP
    nq, spb = t // tq, page_ids.shape[1] // pps
    if bsz * nq != bs * spb:
        return (_attn_prompt(q, k, v, lam, subln, scale_out, tq, tk),
                _attn_decode(qbig, kt_new, v_new, cache_kt, cache_v, page_ids, lam, subln, scale_out))
    heads = w // LANES
    tpad = rows // (2 * A_HEADS)
    seq = lambda b, i, pt: ((b * nq + i) // spb, 0, 0)

    def page_spec(n):
        def index(b, i, pt):
            step = b * nq + i
            return (pt[step // spb, (step % spb) * pps + n], 0, 0)
        return pl.BlockSpec((1, w, PAGE_SIZE), index)

    grid_spec = pltpu.PrefetchScalarGridSpec(
        num_scalar_prefetch=1,
        grid=(bsz, nq),
        in_specs=[
            pl.BlockSpec(memory_space=pltpu.SMEM),
            pl.BlockSpec((1, tq, w), lambda b, i, pt: (b, i, 0)),
            pl.BlockSpec((1, t, w), lambda b, i, pt: (b, 0, 0), pipeline_mode=pl.Buffered(1)),
            pl.BlockSpec((1, t, w), lambda b, i, pt: (b, 0, 0), pipeline_mode=pl.Buffered(1)),
            pl.BlockSpec((1, LANES), lambda b, i, pt: (0, 0)),
            pl.BlockSpec((1, rows, w), seq),
            pl.BlockSpec((1, w, PAGE_SIZE), seq),
            pl.BlockSpec((1, PAGE_SIZE, w), seq),
        ] + [page_spec(n) for n in range(pps)] + [page_spec(n) for n in range(pps)],
        out_specs=[pl.BlockSpec((1, tq, w), lambda b, i, pt: (b, i, 0)), pl.BlockSpec((1, tpad, w), seq)],
        scratch_shapes=[pltpu.VMEM((heads, 2 * tq, LANES), F32)] * 3 + [pltpu.VMEM((rows, LANES), F32)] * 3,
    )
    return pl.pallas_call(
        functools.partial(_attn_both_kernel, scale_out, tk, pps, spb),
        grid_spec=grid_spec,
        out_shape=[jax.ShapeDtypeStruct((bsz, t, w), BF16), jax.ShapeDtypeStruct((bs, tpad, w), BF16)],
        compiler_params=_cparams("arbitrary", "arbitrary"),
        name="attn_both",
    )(page_ids, lam, q, k, v, subln, qbig, kt_new, v_new, *([cache_kt] * pps), *([cache_v] * pps))


def _merge_kernel(rw_ref, att_ref, gate_ref, x_ref, gt_ref, brw_ref, bra_ref, wo_ref, o_ref):
    d = x_ref.shape[1]
    m = (gate_ref[:, :d] * jnp.dot(rw_ref[...], brw_ref[...], preferred_element_type=F32)
         + gate_ref[:, d:] * jnp.dot(att_ref[...], bra_ref[...], preferred_element_type=F32))
    o_ref[...] = x_ref[...] + gt_ref[0] * jnp.dot(m.astype(BF16), wo_ref[...], preferred_element_type=F32)


def _merge(rw, att, gates, x, gt, brw, bra, wo, tm, tps):
    rows, d = x.shape
    row = lambda i: (i, 0)
    const = lambda i: (0, 0)
    return pl.pallas_call(
        _merge_kernel,
        grid=(rows // tm,),
        in_specs=[
            pl.BlockSpec((tm, rw.shape[1]), row),
            pl.BlockSpec((tm, att.shape[1]), row),
            pl.BlockSpec((tm, gates.shape[1]), row),
            pl.BlockSpec((tm, d), row),
            pl.BlockSpec((1, gt.shape[1], d), lambda i: (i // tps, 0, 0)),
            pl.BlockSpec(brw.shape, const),
            pl.BlockSpec(bra.shape, const),
            pl.BlockSpec(wo.shape, const),
        ],
        out_specs=pl.BlockSpec((tm, d), row),
        out_shape=jax.ShapeDtypeStruct((rows, d), F32),
        compiler_params=_cparams("parallel"),
        name="merge_out",
    )(rw, att, gates, x, gt, brw, bra, wo)


def _mlp_kernel(fc, x_ref, sc_ref, sh_ref, gt_ref, g_ref, w1_ref, w2_ref, o_ref):
    x = x_ref[...]
    hb = _rms_mod(x, g_ref[...], sc_ref[0], sh_ref[0]).astype(BF16)
    acc = jnp.zeros(x.shape, F32)
    for c in range(w1_ref.shape[1] // fc):
        u = jnp.dot(hb, w1_ref[:, c * fc:(c + 1) * fc], preferred_element_type=F32)
        u = jnp.square(jnp.maximum(u, 0.0)).astype(BF16)
        acc = acc + jnp.dot(u, w2_ref[c * fc:(c + 1) * fc, :], preferred_element_type=F32)
    o_ref[...] = x + gt_ref[0] * acc


def _mlp(x, mods, g, w1, w2, tm, tps):
    rows, d = x.shape
    sc, sh, gt = mods
    r_mod = sc.shape[1]
    row = lambda i: (i, 0)
    const = lambda i: (0, 0)
    modspec = pl.BlockSpec((1, r_mod, d), lambda i: (i // tps, 0, 0))
    return pl.pallas_call(
        functools.partial(_mlp_kernel, 1024),
        grid=(rows // tm,),
        in_specs=[pl.BlockSpec((tm, d), row), modspec, modspec, modspec, pl.BlockSpec((1, d), const),
                  pl.BlockSpec(w1.shape, const, pipeline_mode=pl.Buffered(1)),
                  pl.BlockSpec(w2.shape, const, pipeline_mode=pl.Buffered(1))],
        out_specs=pl.BlockSpec((tm, d), row),
        out_shape=jax.ShapeDtypeStruct((rows, d), F32),
        compiler_params=_cparams("parallel"),
        name="mlp",
    )(x, sc, sh, gt, g, w1, w2)


def _rope_tables(pos):
    half = ROT_DIM // 2
    inv = ROPE_THETA ** (-jnp.arange(half, dtype=F32) / half)
    ang = pos.astype(F32)[:, None] * inv[None, :]
    cos, sin = jnp.cos(ang), jnp.sin(ang)
    t = pos.shape[0]
    ones = jnp.ones((t, A_DK - ROT_DIM), F32)
    zeros_h = jnp.zeros((t, half), F32)
    zeros_r = jnp.zeros((t, A_DK - ROT_DIM), F32)
    c = jnp.concatenate([cos, cos, ones], axis=1)
    s1 = jnp.concatenate([zeros_h, sin, zeros_r], axis=1)
    s2 = jnp.concatenate([-sin, zeros_h, zeros_r], axis=1)
    rep = lambda a: jnp.concatenate([a] * (LANES // A_DK), axis=1)
    return rep(c), rep(s1), rep(s2)


def _block_ones2(n, blk):
    i = jnp.arange(n) // blk
    e = (i[:, None] == i[None, :]).astype(BF16)
    return jnp.concatenate([e, e], axis=0)


def _split_mod(mod, per_row_reps):
    parts = jnp.split(mod, 6, axis=-1)
    if per_row_reps is None:
        return [m[:, None, :] for m in parts]
    return [jnp.repeat(m, per_row_reps, axis=0)[None] for m in parts]


def kernel(x_prompt, x_sample, c_prompt, c_sample, cache_k, cache_v, state_wkv, state_shift, page_table, norm1_g,
           norm2_g, ada_w, ada_b, w_in, shift_mu, rw_w0, rw_wb, rw_a0, rw_ab, rw_gb, rw_v0, rw_va, rw_vb, rw_kk,
           rw_ka, rw_rk, rw_ln_w, rw_ln_b, at_qnorm, at_knorm, at_lq1, at_lk1, at_lq2, at_lk2, at_subln, br_rwkv,
           br_attn, w_out, mlp_w1, mlp_w2):
    bp, tp, d = x_prompt.shape
    bs, ts, _ = x_sample.shape
    depth = w_in.shape[0]
    rw = rw_w0.shape[1]
    shift_cols = shift_mu.shape[1]
    qk_w = A_HEADS * 2 * A_DK
    v_w = A_HEADS * 2 * A_DK
    n_phys = cache_k.shape[1]
    n_pages = page_table.shape[1]
    past_len = n_pages * PAGE_SIZE
    r_heads = rw // R_HEAD

    mod_all = _ada_mod(jnp.concatenate([c_prompt, c_sample], axis=0), ada_w, ada_b)
    rope_p = _rope_tables(jnp.arange(tp))
    rope_s = tuple(jnp.tile(a, (bs, 1)) for a in _rope_tables(past_len + jnp.arange(ts)))
    e_qk = _block_ones2(qk_w, A_DK)
    e_rw = _block_ones2(rw, R_HEAD)
    ckt = jnp.transpose(cache_k, (0, 1, 3, 4, 5, 2)).reshape(depth * n_phys, qk_w, PAGE_SIZE)
    cv = cache_v.reshape(depth * n_phys, PAGE_SIZE * A_HEADS, v_w // A_HEADS)
    tpad = SUBLANES
    lane = jnp.arange(qk_w)
    q_own = ((lane // LANES)[None, None, :] == jnp.arange(A_HEADS)[:, None, None]) & (
        ((lane // A_DK) % 2)[None, None, :] == jnp.arange(2)[None, :, None])

    xp = x_prompt.reshape(bp * tp, d)
    xs = x_sample.reshape(bs * ts, d)
    rows_s = bs * ts
    outs = {n: [] for n in ("ks", "vs", "sp", "ss", "shp", "shs")}
    vfp = vfs = None
    for l in range(depth):
        lam_init = 0.8 - 0.6 * math.exp(-0.3 * l)
        lam = (jnp.exp(jnp.sum(at_lq1[l] * at_lk1[l]).astype(F32)) - jnp.exp(jnp.sum(at_lq2[l] * at_lk2[l]).astype(F32))
               + lam_init).reshape(1)
        pad_w = jnp.zeros((A_RANK, rw), F32)
        p = dict(
            shift_mu=shift_mu[l][None], rw_w0=rw_w0[l][None], rw_a0=rw_a0[l][None],
            wb_pad=jnp.concatenate([rw_wb[l], pad_w], axis=0).astype(BF16),
            ab_pad=jnp.concatenate([jnp.zeros((W_RANK, rw), F32), rw_ab[l]], axis=0).astype(BF16),
            rw_gb=rw_gb[l].astype(BF16), rw_kk=rw_kk[l][None], rw_ka=rw_ka[l][None])
        if l > 0:
            p.update(rw_v0=rw_v0[l - 1][None], rw_va=rw_va[l - 1].astype(BF16), rw_vb=rw_vb[l - 1].astype(BF16))
        w_in_b = w_in[l].astype(BF16)
        qn = jnp.tile(at_qnorm[l], qk_w // A_DK)[None]
        kn = jnp.tile(at_knorm[l], qk_w // A_DK)[None]
        subln = at_subln[l][None]
        rk = rw_rk[l].reshape(1, rw)
        lnw, lnb = rw_ln_w[l][None], rw_ln_b[l][None]
        brw, bra, wo = br_rwkv[l].astype(BF16), br_attn[l].astype(BF16), w_out[l].astype(BF16)
        w1, w2 = mlp_w1[l].astype(BF16), mlp_w2[l].astype(BF16)
        g1, g2 = norm1_g[l][None], norm2_g[l][None]
        scale_out = 1.0 - lam_init

        sh1, sc1, gt1, sh2, sc2, gt2 = _split_mod(mod_all[l, :bp], None)
        zs, q, kt_all, kb, v_all, vb, gates = _inproj(
            xp, (sc1, sh1), g1, w_in_b, qn, kn, e_qk, rope_p, shift_cols, qk_w, v_w, TM_INPROJ, tp // TM_INPROJ,
            stack=(l, depth, None if l == 0 else (kt_all, v_all)))
        sh1s, sc1s, gt1s, sh2s, sc2s, gt2s = _split_mod(mod_all[l, bp:], ts)
        zs_s, q_s, k_s, _, v_s, _, gates_s = _inproj(xs, (sc1s, sh1s), g1, w_in_b, qn, kn, e_qk, rope_s, shift_cols, qk_w,
                                                    v_w, rows_s, 1)
        q3 = jnp.pad(q_s.reshape(bs, ts, qk_w), ((0, 0), (0, tpad - ts), (0, 0)))
        qbig = jnp.where(q_own[None, :, :, None, :], q3[:, None, None], 0).reshape(bs, 2 * A_HEADS * tpad, qk_w)
        pad_new = lambda a: jnp.pad(a.reshape(bs, ts, -1), ((0, 0), (0, PAGE_SIZE - ts), (0, 0)))
        att, att_s = _attention(q.reshape(bp, tp, qk_w), kb.reshape(bp, tp, qk_w), vb.reshape(bp, tp, v_w), qbig,
                                jnp.swapaxes(pad_new(k_s), 1, 2), pad_new(v_s), ckt, cv, page_table + l * n_phys, lam,
                                subln, scale_out)
        att = att.reshape(bp * tp, v_w)
        res = _rwkv(zs.reshape(bp, tp, shift_cols), jnp.zeros((bp, 1, shift_cols), F32), p, e_rw, rk, lnw, lnb,
                    jnp.zeros((bp, r_heads // 2, LANES, LANES), F32), vfp, CHUNK, WKV_SEQS_PER_STEP)
        rwo, zfin = res[:2]
        if l == 0:
            vfp = res[2]
        x1 = _merge(rwo.reshape(bp * tp, rw), att, gates, xp, gt1, brw, bra, wo, TM_MERGE, tp // TM_MERGE)
        xp = _mlp(x1, (sc2, sh2, gt2), g2, w1, w2, TM_MLP, tp // TM_MLP)
        outs["sp"].append(_blockdiag_to_state(zfin))
        outs["shp"].append(zs.reshape(bp, tp, shift_cols)[:, -1])

        att = att_s[:, :ts].reshape(rows_s, v_w)
        zs3 = zs_s.reshape(bs, ts, shift_cols)
        res = _rwkv(jnp.pad(zs3, ((0, 0), (0, CHUNK - ts), (0, 0))), state_shift[l][:, None, :], p, e_rw, rk, lnw, lnb,
                    _state_to_blockdiag(state_wkv[l]), vfs, ts, WKV_SEQS_PER_STEP)
        rwo, zfin = res[:2]
        if l == 0:
            vfs = res[2]
        rwo = rwo[:, :ts].reshape(rows_s, rw)
        x1 = _merge(rwo, att, gates_s, xs, gt1s, brw, bra, wo, rows_s, 1)
        xs = _mlp(x1, (sc2s, sh2s, gt2s), g2, w1, w2, rows_s, 1)
        outs["ks"].append(k_s.reshape(bs, ts, A_HEADS, 2, A_DK))
        outs["vs"].append(v_s.reshape(bs, ts, A_HEADS, 2 * A_DK))
        outs["ss"].append(_blockdiag_to_state(zfin))
        outs["shs"].append(zs3[:, -1])

    st = lambda n: jnp.stack(outs[n])
    new_k_prompt = jnp.transpose(kt_all.reshape(depth, bp, A_HEADS, 2, A_DK, tp), (0, 1, 5, 2, 3, 4))
    new_v_prompt = v_all.reshape(depth, bp, tp, A_HEADS, 2 * A_DK)
    return (xp.reshape(bp, tp, d), xs.reshape(bs, ts, d), new_k_prompt, new_v_prompt, st("ks"), st("vs"), st("sp"), st("ss"),
            st("shp"), st("shs"))
```

```python
import functools
import math

import jax
import jax.numpy as jnp
from jax import lax
from jax.experimental import pallas as pl
from jax.experimental.pallas import tpu as pltpu

F32, BF16 = jnp.float32, jnp.bfloat16

R_HEAD = 64
GN_EPS = 64e-5
A_HEADS = 4
A_DK = 64
ROT_DIM = A_DK // 4
ROPE_THETA = 500000.0
NORM_EPS = 1e-6
PAGE_SIZE = 128
W_RANK, A_RANK, G_RANK = 64, 64, 128

LANES = 128
SUBLANES = 8
VMEM_LIMIT = 56 * 1024 * 1024

CHUNK = 64
WKV_SEQS_PER_STEP = 4
TM_INPROJ, TM_MERGE, TM_MLP = 512, 512, 512
ATTN_TQ, ATTN_TK = 256, 512
PAGES_PER_STEP = 32

NN = (((1,), (0,)), ((), ()))
NT = (((1,), (1,)), ((), ()))


def _cparams(*sem):
    return pltpu.CompilerParams(dimension_semantics=sem, vmem_limit_bytes=VMEM_LIMIT)


def _dot(a, b, dims=NN):
    return lax.dot_general(a.astype(BF16), b.astype(BF16), dims, preferred_element_type=F32)


def _split2(x):
    hi = x.astype(BF16)
    lo = (x - hi.astype(F32)).astype(BF16)
    return hi, lo


def _mm3(a, b, dims=NN):
    a_hi, a_lo = _split2(a)
    b_hi, b_lo = _split2(b)
    return lax.dot_general(jnp.concatenate([a_hi, a_lo, a_hi], axis=1),
                           jnp.concatenate([b_hi, b_hi, b_lo], axis=dims[0][1][0]), dims,
                           preferred_element_type=F32)


def _seg(x, e2):
    return jnp.dot(jnp.concatenate(_split2(x), axis=1), e2, preferred_element_type=F32)


def _sigmoid(x):
    return 0.5 * jnp.tanh(0.5 * x) + 0.5


def _rms_mod(x, g, sc, sh):
    y = x * lax.rsqrt(jnp.mean(x * x, axis=-1, keepdims=True) + NORM_EPS) * g
    return y * (1.0 + sc) + sh


def _ada_kernel(c_ref, w_ref, b_ref, o_ref):
    c = c_ref[...]
    o_ref[0] = _dot(c * _sigmoid(c), w_ref[0]) + b_ref[0]


def _ada_mod(c_all, ada_w, ada_b):
    depth, d, n = ada_w.shape
    rows = c_all.shape[0]
    tn = d
    return pl.pallas_call(
        _ada_kernel,
        grid=(depth, n // tn),
        in_specs=[
            pl.BlockSpec((rows, d), lambda l, j: (0, 0)),
            pl.BlockSpec((1, d, tn), lambda l, j: (l, 0, j)),
            pl.BlockSpec((1, 1, tn), lambda l, j: (l, 0, j)),
        ],
        out_specs=pl.BlockSpec((1, rows, tn), lambda l, j: (l, 0, j)),
        out_shape=jax.ShapeDtypeStruct((depth, rows, n), F32),
        compiler_params=_cparams("parallel", "parallel"),
        name="ada_mod",
    )(c_all, ada_w, ada_b.reshape(depth, 1, n))


def _inproj_kernel(shift_cols, qk_w, v_w, stacked, *refs):
    x_ref, sc_ref, sh_ref, g_ref, w_ref, qn_ref, kn_ref, e_ref, cos_ref, s1_ref, s2_ref = refs[-18:-7]
    zs_ref, q_ref, k_ref, kb_ref, v_ref, vb_ref, gate_ref = refs[-7:]
    hb = _rms_mod(x_ref[...], g_ref[...], sc_ref[0], sh_ref[0]).astype(BF16)
    o = 0
    zs_ref[...] = jnp.dot(hb, w_ref[:, o:o + shift_cols], preferred_element_type=F32)
    o += shift_cols
    reps = qk_w // LANES
    cos = jnp.concatenate([cos_ref[...]] * reps, axis=1)
    s1 = jnp.concatenate([s1_ref[...]] * reps, axis=1)
    s2 = jnp.concatenate([s2_ref[...]] * reps, axis=1)

    def norm_rope(z, gain):
        ms = _seg(z * z, e_ref[...]) * (1.0 / A_DK)
        y = z * lax.rsqrt(ms + NORM_EPS) * gain
        return y * cos + pltpu.roll(y, ROT_DIM // 2, 1) * s1 + pltpu.roll(y, qk_w - ROT_DIM // 2, 1) * s2

    q = norm_rope(jnp.dot(hb, w_ref[:, o:o + qk_w], preferred_element_type=F32), qn_ref[...])
    q_ref[...] = (q * (A_DK ** -0.5)).astype(BF16)
    o += qk_w
    k = norm_rope(jnp.dot(hb, w_ref[:, o:o + qk_w], preferred_element_type=F32), kn_ref[...])
    if stacked:
        k_ref[0, 0] = k.T
    else:
        k_ref[...] = k
    kb_ref[...] = k.astype(BF16)
    o += qk_w
    v = jnp.dot(hb, w_ref[:, o:o + v_w], preferred_element_type=F32)
    if stacked:
        for h in range(v_w // LANES):
            v_ref[0, pl.ds(h, v.shape[0], stride=v_w // LANES), :] = v[:, h * LANES:(h + 1) * LANES]
    else:
        v_ref[...] = v
    vb_ref[...] = v.astype(BF16)
    o += v_w
    gate_ref[...] = _sigmoid(jnp.dot(hb, w_ref[:, o:], preferred_element_type=F32))


def _inproj(x, mods, g, w_bf, qn, kn, e_qk, rope, shift_cols, qk_w, v_w, tm, tps, stack=None):
    rows, d = x.shape
    in_cols = w_bf.shape[1]
    gate_w = in_cols - shift_cols - 2 * qk_w - v_w
    sc, sh = mods
    r_mod = sc.shape[1]
    row = lambda i: (i, 0)
    const = lambda i: (0, 0)
    modmap = lambda i: (i // tps, 0, 0)
    ropemap = lambda i: (i % tps, 0)
    outs = [(shift_cols, F32), (qk_w, BF16), (qk_w, F32), (qk_w, BF16), (v_w, F32), (v_w, BF16), (gate_w, F32)]
    out_specs = [pl.BlockSpec((tm, w), row) for w, _ in outs]
    out_shape = [jax.ShapeDtypeStruct((rows, w), dt) for w, dt in outs]
    extra_in, extra_specs, aliases = [], [], {}
    if stack is not None:
        layer, depth, prev = stack
        heads = v_w // LANES
        seq_len = tm * tps
        out_specs[2] = pl.BlockSpec((1, 1, qk_w, tm), lambda i: (layer, i // tps, 0, i % tps))
        out_shape[2] = jax.ShapeDtypeStruct((depth, rows // seq_len, qk_w, seq_len), F32)
        out_specs[4] = pl.BlockSpec((1, tm * heads, LANES), lambda i: (layer, i, 0))
        out_shape[4] = jax.ShapeDtypeStruct((depth, rows * heads, LANES), F32)
        if prev is not None:
            extra_in = list(prev)
            extra_specs = [pl.BlockSpec(memory_space=pl.ANY)] * 2
            aliases = {0: 2, 1: 4}
    return pl.pallas_call(
        functools.partial(_inproj_kernel, shift_cols, qk_w, v_w, stack is not None),
        grid=(rows // tm,),
        input_output_aliases=aliases,
        in_specs=extra_specs + [
            pl.BlockSpec((tm, d), row),
            pl.BlockSpec((1, r_mod, d), modmap),
            pl.BlockSpec((1, r_mod, d), modmap),
            pl.BlockSpec((1, d), const),
            pl.BlockSpec((d, in_cols), const, pipeline_mode=pl.Buffered(1)),
            pl.BlockSpec((1, qk_w), const),
            pl.BlockSpec((1, qk_w), const),
            pl.BlockSpec((2 * qk_w, qk_w), const),
            pl.BlockSpec((tm, LANES), ropemap),
            pl.BlockSpec((tm, LANES), ropemap),
            pl.BlockSpec((tm, LANES), ropemap),
        ],
        out_specs=out_specs,
        out_shape=out_shape,
        compiler_params=_cparams("parallel"),
        name="inproj",
    )(*extra_in, x, sc, sh, g, w_bf, qn, kn, e_qk, *rope)


def _rwkv_kernel(has_vres, valid, *refs):
    (zs_ref, first_ref, mu_ref, w0_ref, wb_ref, a0_ref, ab_ref, gb_ref, kk_ref, ka_ref, e_ref, rk_ref, lnw_ref,
     lnb_ref, z0_ref) = refs[:15]
    if has_vres:
        v0_ref, va_ref, vb_ref, vf_ref, y_ref, z_ref, carry_ref = refs[15:]
    else:
        y_ref, z_ref, vout_ref, carry_ref = refs[15:]
    C = CHUNK
    P2 = 2 * C
    nb, _, cols = zs_ref.shape
    rw = rk_ref.shape[1]
    chunk = pl.program_id(1)

    @pl.when(chunk == 0)
    def _():
        z_ref[...] = z0_ref[...]
        for s in range(nb):
            carry_ref[s] = jnp.broadcast_to(first_ref[s], (SUBLANES, cols))

    def prep():
        row_id = lax.broadcasted_iota(jnp.int32, (C, 1), 0)
        zs, prevs = [], []
        for s in range(nb):
            z = zs_ref[s]
            prevs.append(jnp.where(row_id == 0, carry_ref[s, SUBLANES - 1:SUBLANES, :], pltpu.roll(z, 1, 0)))
            carry_ref[s] = z[C - SUBLANES:, :]
            zs.append(z)
        z = jnp.concatenate(zs, axis=0)
        xs = z + (jnp.concatenate(prevs, axis=0) - z) * mu_ref[...]
        r = xs[:, 0:rw]
        k = xs[:, rw:2 * rw]
        v = xs[:, 2 * rw:3 * rw]
        zwa = xs[:, 3 * rw:3 * rw + W_RANK + A_RANK]
        zg = xs[:, 3 * rw + W_RANK + A_RANK:]
        u = -(w0_ref[...] + _dot(jnp.tanh(zwa), wb_ref[...]))
        softplus = jnp.maximum(u, 0.0) + jnp.log(1.0 + jnp.exp(-jnp.abs(u)))
        lw = -jnp.exp(-softplus - 0.5)
        a = _sigmoid(a0_ref[...] + _dot(zwa, ab_ref[...]))
        g = _dot(_sigmoid(zg), gb_ref[...])
        if has_vres:
            vg = _sigmoid(v0_ref[...] + _dot(_dot(v, va_ref[...]), vb_ref[...]))
            v = v + (jnp.concatenate([vf_ref[s] for s in range(nb)], axis=0) - v) * vg
        else:
            for s in range(nb):
                vout_ref[s] = v[s * C:(s + 1) * C]
        kk = k * kk_ref[...]
        kk = kk / jnp.maximum(jnp.sqrt(_seg(kk * kk, e_ref[...])), 1e-12)
        k = k * (1.0 + (a - 1.0) * ka_ref[...])
        b = kk * a
        if valid < C:
            keep = (lax.broadcasted_iota(jnp.int32, (nb * C, 1), 0) & (C - 1)) < valid
            lw, k, v, kk, b = (jnp.where(keep, t, 0.0) for t in (lw, k, v, kk, b))
        return r, lw, k, v, kk, b, g

    ri = lax.broadcasted_iota(jnp.int32, (P2, P2), 0)
    ci = lax.broadcasted_iota(jnp.int32, (P2, P2), 1)
    strict = ci < ri
    incl = ci <= ri
    eye = (ci == ri).astype(F32)
    blk = lambda s: (ri >> s) == (ci >> s)
    e_pair = blk(6).astype(BF16)
    e_head = jnp.concatenate([e_pair, e_pair], axis=0)
    ti = lax.broadcasted_iota(jnp.int32, (C, C), 0)
    tj = lax.broadcasted_iota(jnp.int32, (C, C), 1)
    tril1 = (tj <= ti).astype(BF16)
    head_a = lax.broadcasted_iota(jnp.int32, (C, LANES), 1) < R_HEAD

    def stack(x):
        return jnp.concatenate([jnp.where(head_a, x, 0.0), jnp.where(head_a, 0.0, x)], axis=0)

    units = [(s, p) for s in range(nb) for p in range(rw // LANES)]
    pairs = range(len(units))
    sls = [slice(p * LANES, (p + 1) * LANES) for _, p in units]
    mm = _dot
    cat = jnp.concatenate
    tok = prep()
    ld = lambda x: [x[s * C:(s + 1) * C, sl] for (s, _), sl in zip(units, sls)]
    r, lw, k, v, kp, b, gate = (ld(x) for x in tok)

    def split3(x):
        x0 = x.astype(BF16)
        x1 = x - x0.astype(F32)
        x1b = x1.astype(BF16)
        x2b = (x1 - x1b.astype(F32)).astype(BF16)
        return cat([x0, x1b, x2b], axis=0)

    logp_all = jnp.dot(cat([tril1] * 3, axis=1), cat([split3(x) for x in lw], axis=1), preferred_element_type=F32)
    logp = [logp_all[:, p * LANES:(p + 1) * LANES] for p in pairs]
    logpc = [x[C - 1:C, :] for x in logp]
    e_neg = [jnp.exp(-x) for x in logp]
    e_end = [jnp.exp(xc - x) for xc, x in zip(logpc, logp)]
    rt = [stack(r[p] * jnp.exp(logp[p])) for p in pairs]
    kpt = [stack(kp[p] * jnp.exp(logp[p] - lw[p])) for p in pairs]
    v_s = [stack(v[p]) for p in pairs]
    gram = [_dot(cat([kpt[p], rt[p]], axis=0), cat([stack(b[p] * e_neg[p]), stack(k[p] * e_neg[p])], axis=0), NT)
            for p in pairs]
    lb = [jnp.where(strict, g[:P2, :P2], 0.0) for g in gram]
    lk = [jnp.where(strict, g[:P2, P2:], 0.0) for g in gram]
    mb = [jnp.where(incl, g[P2:, :P2], 0.0) for g in gram]
    mk = [jnp.where(incl, g[P2:, P2:], 0.0) for g in gram]
    x = [mm(lk[p], v_s[p]) for p in pairs]
    n1 = [jnp.where(blk(3), -m, 0.0) for m in lb]
    n2 = [mm(m, m) for m in n1]
    t = [mm(eye + a, eye + c) for a, c in zip(n1, n2)]
    n4 = [mm(m, m) for m in n2]
    t = [mm(a, eye + c) for a, c in zip(t, n4)]
    for s in (3, 4, 5):
        off_mask = blk(s + 1) & jnp.logical_not(blk(s))
        w = [mm(jnp.where(off_mask, lb[p], 0.0), t[p]) for p in pairs]
        t = [t[p] - mm(t[p], w[p]) for p in pairs]
    wu = [-mm(t[p], cat([kpt[p], x[p]], axis=1)) for p in pairs]
    lhs = [cat([cat([mb[p], mk[p]], axis=1),
                cat([stack(b[p] * e_end[p]).T, stack(k[p] * e_end[p]).T], axis=1)], axis=0) for p in pairs]
    rhs = [cat([wu[p], cat([jnp.zeros_like(v_s[p]), v_s[p]], axis=1)], axis=0) for p in pairs]
    big = [mm(lhs[p], rhs[p]) for p in pairs]
    qa = [cat([rt[p] + big[p][:P2, :LANES], eye * jnp.exp(logpc[p]) + big[p][P2:, :LANES]], axis=0) for p in pairs]
    qaz = [_mm3(qa[p], z_ref[units[p]]) for p in pairs]
    for p in pairs:
        z_ref[units[p]] = qaz[p][P2:] + big[p][P2:, LANES:]
    ys = [qaz[p][:P2] + big[p][:P2, LANES:] for p in pairs]
    y = cat([x[:C] + x[C:] for x in ys], axis=0)
    rkr = cat([r[p] * k[p] * rk_ref[:, sls[p]] for p in pairs], axis=0)
    sums = _seg(cat([y, rkr], axis=0), e_head)
    n_rows = y.shape[0]
    d = y - sums[:n_rows] * (1.0 / R_HEAD)
    inv_std = lax.rsqrt(_seg(d * d, e_head) * (1.0 / R_HEAD) + GN_EPS)
    for p in pairs:
        seq = units[p][0]
        rows = slice(p * C, (p + 1) * C)
        yn = d[rows] * inv_std[rows] * lnw_ref[:, sls[p]] + lnb_ref[:, sls[p]]
        bonus = sums[n_rows:][rows] * v[p]
        y_ref[seq, :, sls[p]] = ((yn + bonus) * gate[p]).astype(y_ref.dtype)


def _rwkv(zs, first, p, e_rw, rk, lnw, lnb, z0, v_first, valid, nb):
    bsz, t, cols = zs.shape
    rw = rk.shape[1]
    assert bsz % nb == 0 and t % CHUNK == 0 and rw % LANES == 0, (zs.shape, nb)
    n_pairs = rw // LANES
    has_vres = v_first is not None
    seq = lambda bb, c: (bb, c, 0)
    per_b = lambda bb, c: (bb, 0, 0)
    const = lambda bb, c: (0, 0)
    zmap = lambda bb, c: (bb, 0, 0, 0)
    full = lambda a: pl.BlockSpec(a.shape, const)
    params = [p["shift_mu"], p["rw_w0"], p["wb_pad"], p["rw_a0"], p["ab_pad"], p["rw_gb"], p["rw_kk"], p["rw_ka"],
              e_rw, rk, lnw, lnb]
    args = [zs, first] + params + [z0]
    in_specs = ([pl.BlockSpec((nb, CHUNK, cols), seq), pl.BlockSpec((nb, 1, cols), per_b)] + [full(a) for a in params]
                + [pl.BlockSpec((nb, n_pairs, LANES, LANES), zmap)])
    out_specs = [pl.BlockSpec((nb, CHUNK, rw), seq), pl.BlockSpec((nb, n_pairs, LANES, LANES), zmap)]
    out_shape = [jax.ShapeDtypeStruct((bsz, t, rw), BF16), jax.ShapeDtypeStruct((bsz, n_pairs, LANES, LANES), F32)]
    if has_vres:
        extra = [p["rw_v0"], p["rw_va"], p["rw_vb"]]
        args += extra + [v_first]
        in_specs += [full(a) for a in extra] + [pl.BlockSpec((nb, CHUNK, rw), seq)]
    else:
        out_specs.append(pl.BlockSpec((nb, CHUNK, rw), seq))
        out_shape.append(jax.ShapeDtypeStruct((bsz, t, rw), F32))
    return pl.pallas_call(
        functools.partial(_rwkv_kernel, has_vres, valid),
        grid=(bsz // nb, t // CHUNK),
        in_specs=in_specs,
        out_specs=out_specs,
        out_shape=out_shape,
        scratch_shapes=[pltpu.VMEM((nb, SUBLANES, cols), F32)],
        compiler_params=_cparams("parallel", "arbitrary"),
        name="rwkv",
    )(*args)


def _state_to_blockdiag(s):
    bsz, h, n, _ = s.shape
    zt = jnp.swapaxes(s, -1, -2).reshape(bsz, h // 2, 2, n, n)
    zero = jnp.zeros_like(zt[:, :, 0])
    top = jnp.concatenate([zt[:, :, 0], zero], axis=-1)
    bot = jnp.concatenate([zero, zt[:, :, 1]], axis=-1)
    return jnp.concatenate([top, bot], axis=-2)


def _blockdiag_to_state(z):
    n = R_HEAD
    za = z[:, :, :n, :n]
    zb = z[:, :, n:, n:]
    s = jnp.stack([za, zb], axis=2)
    s = s.reshape(z.shape[0], -1, n, n)
    return jnp.swapaxes(s, -1, -2)


def _attn_kernel(scale_out, tk, *refs):
    _attn_body(scale_out, tk, pl.program_id(1), *refs)


def _attn_body(scale_out, tk, qi, lam_ref, q_ref, k_ref, v_ref, sg_ref, o_ref, m_ref, l_ref, acc_ref):
    tq = q_ref.shape[1]
    heads = range(q_ref.shape[2] // LANES)
    hsl = [slice(h * LANES, (h + 1) * LANES) for h in heads]
    jd = qi // (tk // tq)
    first = lax.broadcasted_iota(jnp.int32, (tq, LANES), 1) < A_DK
    q2 = []
    for h in heads:
        q = q_ref[0, :, hsl[h]]
        zero = jnp.zeros_like(q)
        q2.append(jnp.concatenate([jnp.where(first, q, zero), jnp.where(first, zero, q)], axis=0))
    m_ref[...] = jnp.full(m_ref.shape, -jnp.inf, F32)
    l_ref[...] = jnp.zeros(l_ref.shape, F32)
    acc_ref[...] = jnp.zeros(acc_ref.shape, F32)
    reps = tk // LANES

    def block(start, mask):
        s = [lax.dot_general(q2[h], k_ref[0, pl.ds(start, tk), hsl[h]], NT, preferred_element_type=F32)
             for h in heads]
        if mask is not None:
            s = [jnp.where(mask, x, -jnp.inf) for x in s]
        m_prev = [m_ref[h] for h in heads]
        m_next = [jnp.maximum(m_prev[h], jnp.max(s[h], axis=1, keepdims=True)) for h in heads]
        p = [jnp.exp(s[h] - jnp.concatenate([m_next[h]] * reps, axis=1)) for h in heads]
        alpha = [jnp.exp(m_prev[h] - m_next[h]) for h in heads]
        for h in heads:
            l_ref[h] = alpha[h] * l_ref[h] + jnp.sum(p[h], axis=1, keepdims=True)
            m_ref[h] = m_next[h]
        pv = [jnp.dot(p[h].astype(BF16), v_ref[0, pl.ds(start, tk), hsl[h]], preferred_element_type=F32)
              for h in heads]
        for h in heads:
            acc_ref[h] = alpha[h] * acc_ref[h] + pv[h]

    def body(j, carry):
        block(pl.multiple_of(j * tk, tk), None)
        return carry

    lax.fori_loop(0, jd, body, 0)
    row = lax.broadcasted_iota(jnp.int32, (2 * tq, tk), 0)
    col = lax.broadcasted_iota(jnp.int32, (2 * tq, tk), 1)
    row = jnp.where(row >= tq, row - tq, row) + (qi * tq - jd * tk)
    block(pl.multiple_of(jd * tk, tk), col <= row)
    for h in heads:
        o = acc_ref[h] / l_ref[h]
        att = o[:tq] - lam_ref[0] * o[tq:]
        att = att * lax.rsqrt(jnp.mean(att * att, axis=-1, keepdims=True) + NORM_EPS) * sg_ref[...]
        o_ref[0, :, hsl[h]] = (att * scale_out).astype(o_ref.dtype)


def _attn_prompt(q, k, v, lam, subln, scale_out, tq, tk):
    bsz, t, w = q.shape
    heads = w // LANES
    return pl.pallas_call(
        functools.partial(_attn_kernel, scale_out, tk),
        grid=(bsz, t // tq),
        in_specs=[
            pl.BlockSpec(memory_space=pltpu.SMEM),
            pl.BlockSpec((1, tq, w), lambda b, i: (b, i, 0)),
            pl.BlockSpec((1, t, w), lambda b, i: (b, 0, 0)),
            pl.BlockSpec((1, t, w), lambda b, i: (b, 0, 0)),
            pl.BlockSpec((1, LANES), lambda b, i: (0, 0)),
        ],
        out_specs=pl.BlockSpec((1, tq, w), lambda b, i: (b, i, 0)),
        out_shape=jax.ShapeDtypeStruct((bsz, t, w), BF16),
        scratch_shapes=[pltpu.VMEM((heads, 2 * tq, LANES), F32)] * 3,
        compiler_params=_cparams("parallel", "arbitrary"),
        name="attn_prompt",
    )(lam, q, k, v, subln)


def _decode_kernel(scale_out, n_pages_step, pt_ref, *refs):
    _decode_body(scale_out, n_pages_step, pl.program_id(1), pl.num_programs(1), *refs)


def _decode_body(scale_out, n_pages_step, j, n_steps, lam_ref, q_ref, kn_ref, vn_ref, *refs):
    k_refs = refs[:n_pages_step]
    v_refs = refs[n_pages_step:2 * n_pages_step]
    sg_ref, o_ref, m_ref, l_ref, acc_ref = refs[2 * n_pages_step:]
    rows = q_ref.shape[1]
    hrows = rows // A_HEADS
    tpad = hrows // 2

    @pl.when(j == 0)
    def _():
        m_ref[...] = jnp.full(m_ref.shape, -jnp.inf, F32)
        l_ref[...] = jnp.zeros(l_ref.shape, F32)
        acc_ref[...] = jnp.zeros(acc_ref.shape, F32)

    q = q_ref[0]

    def update(s_list, v_fn):
        m_prev = m_ref[...]
        m_cur = s_list[0].max(axis=1, keepdims=True)
        for s in s_list[1:]:
            m_cur = jnp.maximum(m_cur, s.max(axis=1, keepdims=True))
        m_next = jnp.maximum(m_prev, m_cur)
        alpha = jnp.exp(m_prev - m_next)
        l_new = alpha * l_ref[...]
        p_list = []
        for s in s_list:
            p = jnp.exp(s - m_next)
            l_new = l_new + jnp.sum(p, axis=1, keepdims=True)
            p_list.append(p.astype(BF16))
        l_ref[...] = l_new
        m_ref[...] = m_next
        for h in range(A_HEADS):
            hs = slice(h * hrows, (h + 1) * hrows)
            acc = alpha[hs] * acc_ref[hs, :]
            for i, p in enumerate(p_list):
                acc = acc + jnp.dot(p[hs], v_fn(i, h).astype(BF16), preferred_element_type=F32)
            acc_ref[hs, :] = acc

    s_list = [jnp.dot(q, kr[0].astype(BF16), preferred_element_type=F32) for kr in k_refs]
    update(s_list, lambda i, h: v_refs[i][0, pl.ds(h, PAGE_SIZE, stride=A_HEADS), :])

    @pl.when(j == n_steps - 1)
    def _():
        s = jnp.dot(q, kn_ref[0].astype(BF16), preferred_element_type=F32)
        tok = lax.broadcasted_iota(jnp.int32, s.shape, 0) & (tpad - 1)
        key = lax.broadcasted_iota(jnp.int32, s.shape, 1)
        update([jnp.where(key <= tok, s, -jnp.inf)], lambda i, h: vn_ref[0, :, h * LANES:(h + 1) * LANES])
        inv_l = 1.0 / l_ref[...]
        for h in range(A_HEADS):
            r0 = slice(h * hrows, h * hrows + tpad)
            r1 = slice(h * hrows + tpad, (h + 1) * hrows)
            att = acc_ref[r0, :] * inv_l[r0] - lam_ref[0] * (acc_ref[r1, :] * inv_l[r1])
            att = att * lax.rsqrt(jnp.mean(att * att, axis=-1, keepdims=True) + NORM_EPS) * sg_ref[...]
            o_ref[0, :, h * LANES:(h + 1) * LANES] = (att * scale_out).astype(o_ref.dtype)


def _attn_decode(qbig, kt_new, v_new, cache_kt, cache_v, page_ids, lam, subln, scale_out):
    bsz, rows, w = qbig.shape
    n_pages = page_ids.shape[1]
    pps = PAGES_PER_STEP
    tpad = rows // (2 * A_HEADS)

    def page_spec(i):
        return pl.BlockSpec((1, w, PAGE_SIZE), lambda b, j, pt: (pt[b, j * pps + i], 0, 0))

    per_b = lambda b, j, pt: (b, 0, 0)
    grid_spec = pltpu.PrefetchScalarGridSpec(
        num_scalar_prefetch=1,
        grid=(bsz, n_pages // pps),
        in_specs=[
            pl.BlockSpec(memory_space=pltpu.SMEM),
            pl.BlockSpec((1, rows, w), per_b),
            pl.BlockSpec((1, w, PAGE_SIZE), per_b),
            pl.BlockSpec((1, PAGE_SIZE, w), per_b),
        ] + [page_spec(i) for i in range(pps)] + [page_spec(i) for i in range(pps)]
        + [pl.BlockSpec((1, LANES), lambda b, j, pt: (0, 0))],
        out_specs=pl.BlockSpec((1, tpad, w), per_b),
        scratch_shapes=[pltpu.VMEM((rows, LANES), F32)] * 3,
    )
    return pl.pallas_call(
        functools.partial(_decode_kernel, scale_out, pps),
        grid_spec=grid_spec,
        out_shape=jax.ShapeDtypeStruct((bsz, tpad, w), BF16),
        compiler_params=_cparams("parallel", "arbitrary"),
        name="attn_decode",
    )(page_ids, lam, qbig, kt_new, v_new, *([cache_kt] * pps), *([cache_v] * pps), subln)


def _attn_both_kernel(scale_out, tk, pps, steps_per_seq, pt_ref, lam_ref, q_ref, k_ref, v_ref, sg_ref, qd_ref, kn_ref,
                      vn_ref, *refs):
    page_refs = refs[:2 * pps]
    o_ref, od_ref, m_ref, l_ref, acc_ref, md_ref, ld_ref, accd_ref = refs[2 * pps:]
    qi = pl.program_id(1)
    _attn_body(scale_out, tk, qi, lam_ref, q_ref, k_ref, v_ref, sg_ref, o_ref, m_ref, l_ref, acc_ref)
    step = pl.program_id(0) * pl.num_programs(1) + qi
    _decode_body(scale_out, pps, step % steps_per_seq, steps_per_seq, lam_ref, qd_ref, kn_ref, vn_ref, *page_refs,
                 sg_ref, od_ref, md_ref, ld_ref, accd_ref)


def _attention(q, k, v, qbig, kt_new, v_new, cache_kt, cache_v, page_ids, lam, subln, scale_out):
    bsz, t, w = q.shape
    bs, rows, _ = qbig.shape
    tq, tk, pps = ATTN_TQ, ATTN_TK, PAGES_PER_STEP
    nq, spb = t // tq, page_ids.shape[1] // pps
    if bsz * nq != bs * spb:
        return (_attn_prompt(q, k, v, lam, subln, scale_out, tq, tk),
                _attn_decode(qbig, kt_new, v_new, cache_kt, cache_v, page_ids, lam, subln, scale_out))
    heads = w // LANES
    tpad = rows // (2 * A_HEADS)
    seq = lambda b, i, pt: ((b * nq + i) // spb, 0, 0)

    def page_spec(n):
        return pl.BlockSpec((1, w, PAGE_SIZE), lambda b, i, pt: (pt[b * nq + i, n], 0, 0))

    grid_spec = pltpu.PrefetchScalarGridSpec(
        num_scalar_prefetch=1,
        grid=(bsz, nq),
        in_specs=[
            pl.BlockSpec(memory_space=pltpu.SMEM),
            pl.BlockSpec((1, tq, w), lambda b, i, pt: (b, i, 0)),
            pl.BlockSpec((1, t, w), lambda b, i, pt: (b, 0, 0), pipeline_mode=pl.Buffered(1)),
            pl.BlockSpec((1, t, w), lambda b, i, pt: (b, 0, 0), pipeline_mode=pl.Buffered(1)),
            pl.BlockSpec((1, LANES), lambda b, i, pt: (0, 0)),
            pl.BlockSpec((1, rows, w), seq),
            pl.BlockSpec((1, w, PAGE_SIZE), seq),
            pl.BlockSpec((1, PAGE_SIZE, w), seq),
        ] + [page_spec(n) for n in range(pps)] + [page_spec(n) for n in range(pps)],
        out_specs=[pl.BlockSpec((1, tq, w), lambda b, i, pt: (b, i, 0)), pl.BlockSpec((1, tpad, w), seq)],
        scratch_shapes=[pltpu.VMEM((heads, 2 * tq, LANES), F32)] * 3 + [pltpu.VMEM((rows, LANES), F32)] * 3,
    )
    return pl.pallas_call(
        functools.partial(_attn_both_kernel, scale_out, tk, pps, spb),
        grid_spec=grid_spec,
        out_shape=[jax.ShapeDtypeStruct((bsz, t, w), BF16), jax.ShapeDtypeStruct((bs, tpad, w), BF16)],
        compiler_params=_cparams("arbitrary", "arbitrary"),
        name="attn_both",
    )(page_ids.reshape(bs * spb, pps), lam, q, k, v, subln, qbig, kt_new, v_new, *([cache_kt] * pps),
      *([cache_v] * pps))


def _merge_kernel(rw_ref, att_ref, gate_ref, x_ref, gt_ref, brw_ref, bra_ref, wo_ref, o_ref):
    d = x_ref.shape[1]
    m = (gate_ref[:, :d] * jnp.dot(rw_ref[...], brw_ref[...], preferred_element_type=F32)
         + gate_ref[:, d:] * jnp.dot(att_ref[...], bra_ref[...], preferred_element_type=F32))
    o_ref[...] = x_ref[...] + gt_ref[0] * jnp.dot(m.astype(BF16), wo_ref[...], preferred_element_type=F32)


def _merge(rw, att, gates, x, gt, brw, bra, wo, tm, tps):
    rows, d = x.shape
    row = lambda i: (i, 0)
    const = lambda i: (0, 0)
    return pl.pallas_call(
        _merge_kernel,
        grid=(rows // tm,),
        in_specs=[
            pl.BlockSpec((tm, rw.shape[1]), row),
            pl.BlockSpec((tm, att.shape[1]), row),
            pl.BlockSpec((tm, gates.shape[1]), row),
            pl.BlockSpec((tm, d), row),
            pl.BlockSpec((1, gt.shape[1], d), lambda i: (i // tps, 0, 0)),
            pl.BlockSpec(brw.shape, const),
            pl.BlockSpec(bra.shape, const),
            pl.BlockSpec(wo.shape, const),
        ],
        out_specs=pl.BlockSpec((tm, d), row),
        out_shape=jax.ShapeDtypeStruct((rows, d), F32),
        compiler_params=_cparams("parallel"),
        name="merge_out",
    )(rw, att, gates, x, gt, brw, bra, wo)


def _mlp_kernel(fc, x_ref, sc_ref, sh_ref, gt_ref, g_ref, w1_ref, w2_ref, o_ref):
    x = x_ref[...]
    hb = _rms_mod(x, g_ref[...], sc_ref[0], sh_ref[0]).astype(BF16)
    acc = jnp.zeros(x.shape, F32)
    for c in range(w1_ref.shape[1] // fc):
        u = jnp.dot(hb, w1_ref[:, c * fc:(c + 1) * fc], preferred_element_type=F32)
        u = jnp.square(jnp.maximum(u, 0.0)).astype(BF16)
        acc = acc + jnp.dot(u, w2_ref[c * fc:(c + 1) * fc, :], preferred_element_type=F32)
    o_ref[...] = x + gt_ref[0] * acc


def _mlp(x, mods, g, w1, w2, tm, tps):
    rows, d = x.shape
    sc, sh, gt = mods
    r_mod = sc.shape[1]
    row = lambda i: (i, 0)
    const = lambda i: (0, 0)
    modspec = pl.BlockSpec((1, r_mod, d), lambda i: (i // tps, 0, 0))
    return pl.pallas_call(
        functools.partial(_mlp_kernel, 1024),
        grid=(rows // tm,),
        in_specs=[pl.BlockSpec((tm, d), row), modspec, modspec, modspec, pl.BlockSpec((1, d), const),
                  pl.BlockSpec(w1.shape, const, pipeline_mode=pl.Buffered(1)),
                  pl.BlockSpec(w2.shape, const, pipeline_mode=pl.Buffered(1))],
        out_specs=pl.BlockSpec((tm, d), row),
        out_shape=jax.ShapeDtypeStruct((rows, d), F32),
        compiler_params=_cparams("parallel"),
        name="mlp",
    )(x, sc, sh, gt, g, w1, w2)


def _rope_tables(pos):
    half = ROT_DIM // 2
    inv = ROPE_THETA ** (-jnp.arange(half, dtype=F32) / half)
    ang = pos.astype(F32)[:, None] * inv[None, :]
    cos, sin = jnp.cos(ang), jnp.sin(ang)
    t = pos.shape[0]
    ones = jnp.ones((t, A_DK - ROT_DIM), F32)
    zeros_h = jnp.zeros((t, half), F32)
    zeros_r = jnp.zeros((t, A_DK - ROT_DIM), F32)
    c = jnp.concatenate([cos, cos, ones], axis=1)
    s1 = jnp.concatenate([zeros_h, sin, zeros_r], axis=1)
    s2 = jnp.concatenate([-sin, zeros_h, zeros_r], axis=1)
    rep = lambda a: jnp.concatenate([a] * (LANES // A_DK), axis=1)
    return rep(c), rep(s1), rep(s2)


def _block_ones2(n, blk):
    i = jnp.arange(n) // blk
    e = (i[:, None] == i[None, :]).astype(BF16)
    return jnp.concatenate([e, e], axis=0)


def _split_mod(mod, per_row_reps):
    parts = jnp.split(mod, 6, axis=-1)
    if per_row_reps is None:
        return [m[:, None, :] for m in parts]
    return [jnp.repeat(m, per_row_reps, axis=0)[None] for m in parts]


def kernel(x_prompt, x_sample, c_prompt, c_sample, cache_k, cache_v, state_wkv, state_shift, page_table, norm1_g,
           norm2_g, ada_w, ada_b, w_in, shift_mu, rw_w0, rw_wb, rw_a0, rw_ab, rw_gb, rw_v0, rw_va, rw_vb, rw_kk,
           rw_ka, rw_rk, rw_ln_w, rw_ln_b, at_qnorm, at_knorm, at_lq1, at_lk1, at_lq2, at_lk2, at_subln, br_rwkv,
           br_attn, w_out, mlp_w1, mlp_w2):
    bp, tp, d = x_prompt.shape
    bs, ts, _ = x_sample.shape
    depth = w_in.shape[0]
    rw = rw_w0.shape[1]
    shift_cols = shift_mu.shape[1]
    qk_w = A_HEADS * 2 * A_DK
    v_w = A_HEADS * 2 * A_DK
    n_phys = cache_k.shape[1]
    n_pages = page_table.shape[1]
    past_len = n_pages * PAGE_SIZE
    r_heads = rw // R_HEAD

    mod_all = _ada_mod(jnp.concatenate([c_prompt, c_sample], axis=0), ada_w, ada_b)
    rope_p = _rope_tables(jnp.arange(tp))
    rope_s = tuple(jnp.tile(a, (bs, 1)) for a in _rope_tables(past_len + jnp.arange(ts)))
    e_qk = _block_ones2(qk_w, A_DK)
    e_rw = _block_ones2(rw, R_HEAD)
    ckt = jnp.transpose(cache_k, (0, 1, 3, 4, 5, 2)).reshape(depth * n_phys, qk_w, PAGE_SIZE)
    cv = cache_v.reshape(depth * n_phys, PAGE_SIZE * A_HEADS, v_w // A_HEADS)
    tpad = SUBLANES
    lane = jnp.arange(qk_w)
    q_own = ((lane // LANES)[None, None, :] == jnp.arange(A_HEADS)[:, None, None]) & (
        ((lane // A_DK) % 2)[None, None, :] == jnp.arange(2)[None, :, None])

    xp = x_prompt.reshape(bp * tp, d)
    xs = x_sample.reshape(bs * ts, d)
    rows_s = bs * ts
    outs = {n: [] for n in ("ks", "vs", "sp", "ss", "shp", "shs")}
    vfp = vfs = None
    for l in range(depth):
        lam_init = 0.8 - 0.6 * math.exp(-0.3 * l)
        lam = (jnp.exp(jnp.sum(at_lq1[l] * at_lk1[l]).astype(F32)) - jnp.exp(jnp.sum(at_lq2[l] * at_lk2[l]).astype(F32))
               + lam_init).reshape(1)
        pad_w = jnp.zeros((A_RANK, rw), F32)
        p = dict(
            shift_mu=shift_mu[l][None], rw_w0=rw_w0[l][None], rw_a0=rw_a0[l][None],
            wb_pad=jnp.concatenate([rw_wb[l], pad_w], axis=0).astype(BF16),
            ab_pad=jnp.concatenate([jnp.zeros((W_RANK, rw), F32), rw_ab[l]], axis=0).astype(BF16),
            rw_gb=rw_gb[l].astype(BF16), rw_kk=rw_kk[l][None], rw_ka=rw_ka[l][None])
        if l > 0:
            p.update(rw_v0=rw_v0[l - 1][None], rw_va=rw_va[l - 1].astype(BF16), rw_vb=rw_vb[l - 1].astype(BF16))
        w_in_b = w_in[l].astype(BF16)
        qn = jnp.tile(at_qnorm[l], qk_w // A_DK)[None]
        kn = jnp.tile(at_knorm[l], qk_w // A_DK)[None]
        subln = at_subln[l][None]
        rk = rw_rk[l].reshape(1, rw)
        lnw, lnb = rw_ln_w[l][None], rw_ln_b[l][None]
        brw, bra, wo = br_rwkv[l].astype(BF16), br_attn[l].astype(BF16), w_out[l].astype(BF16)
        w1, w2 = mlp_w1[l].astype(BF16), mlp_w2[l].astype(BF16)
        g1, g2 = norm1_g[l][None], norm2_g[l][None]
        scale_out = 1.0 - lam_init

        sh1, sc1, gt1, sh2, sc2, gt2 = _split_mod(mod_all[l, :bp], None)
        zs, q, kt_all, kb, v_all, vb, gates = _inproj(
            xp, (sc1, sh1), g1, w_in_b, qn, kn, e_qk, rope_p, shift_cols, qk_w, v_w, TM_INPROJ, tp // TM_INPROJ,
            stack=(l, depth, None if l == 0 else (kt_all, v_all)))
        sh1s, sc1s, gt1s, sh2s, sc2s, gt2s = _split_mod(mod_all[l, bp:], ts)
        zs_s, q_s, k_s, _, v_s, _, gates_s = _inproj(xs, (sc1s, sh1s), g1, w_in_b, qn, kn, e_qk, rope_s, shift_cols, qk_w,
                                                    v_w, rows_s, 1)
        q3 = jnp.pad(q_s.reshape(bs, ts, qk_w), ((0, 0), (0, tpad - ts), (0, 0)))
        qbig = jnp.where(q_own[None, :, :, None, :], q3[:, None, None], 0).reshape(bs, 2 * A_HEADS * tpad, qk_w)
        pad_new = lambda a: jnp.pad(a.reshape(bs, ts, -1), ((0, 0), (0, PAGE_SIZE - ts), (0, 0)))
        att, att_s = _attention(q.reshape(bp, tp, qk_w), kb.reshape(bp, tp, qk_w), vb.reshape(bp, tp, v_w), qbig,
                                jnp.swapaxes(pad_new(k_s), 1, 2), pad_new(v_s), ckt, cv, page_table + l * n_phys, lam,
                                subln, scale_out)
        att = att.reshape(bp * tp, v_w)
        res = _rwkv(zs.reshape(bp, tp, shift_cols), jnp.zeros((bp, 1, shift_cols), F32), p, e_rw, rk, lnw, lnb,
                    jnp.zeros((bp, r_heads // 2, LANES, LANES), F32), vfp, CHUNK, WKV_SEQS_PER_STEP)
        rwo, zfin = res[:2]
        if l == 0:
            vfp = res[2]
        x1 = _merge(rwo.reshape(bp * tp, rw), att, gates, xp, gt1, brw, bra, wo, TM_MERGE, tp // TM_MERGE)
        xp = _mlp(x1, (sc2, sh2, gt2), g2, w1, w2, TM_MLP, tp // TM_MLP)
        outs["sp"].append(_blockdiag_to_state(zfin))
        outs["shp"].append(zs.reshape(bp, tp, shift_cols)[:, -1])

        att = att_s[:, :ts].reshape(rows_s, v_w)
        zs3 = zs_s.reshape(bs, ts, shift_cols)
        res = _rwkv(jnp.pad(zs3, ((0, 0), (0, CHUNK - ts), (0, 0))), state_shift[l][:, None, :], p, e_rw, rk, lnw, lnb,
                    _state_to_blockdiag(state_wkv[l]), vfs, ts, WKV_SEQS_PER_STEP)
        rwo, zfin = res[:2]
        if l == 0:
            vfs = res[2]
        rwo = rwo[:, :ts].reshape(rows_s, rw)
        x1 = _merge(rwo, att, gates_s, xs, gt1s, brw, bra, wo, rows_s, 1)
        xs = _mlp(x1, (sc2s, sh2s, gt2s), g2, w1, w2, rows_s, 1)
        outs["ks"].append(k_s.reshape(bs, ts, A_HEADS, 2, A_DK))
        outs["vs"].append(v_s.reshape(bs, ts, A_HEADS, 2 * A_DK))
        outs["ss"].append(_blockdiag_to_state(zfin))
        outs["shs"].append(zs3[:, -1])

    st = lambda n: jnp.stack(outs[n])
    new_k_prompt = jnp.transpose(kt_all.reshape(depth, bp, A_HEADS, 2, A_DK, tp), (0, 1, 5, 2, 3, 4))
    new_v_prompt = v_all.reshape(depth, bp, tp, A_HEADS, 2 * A_DK)
    return (xp.reshape(bp, tp, d), xs.reshape(bs, ts, d), new_k_prompt, new_v_prompt, st("ks"), st("vs"), st("sp"), st("ss"),
            st("shp"), st("shs"))
```

```python
import functools
import math

import jax
import jax.numpy as jnp
from jax import lax
from jax.experimental import pallas as pl
from jax.experimental.pallas import tpu as pltpu

F32, BF16 = jnp.float32, jnp.bfloat16

R_HEAD = 64
GN_EPS = 64e-5
A_HEADS = 4
A_DK = 64
ROT_DIM = A_DK // 4
ROPE_THETA = 500000.0
NORM_EPS = 1e-6
PAGE_SIZE = 128
W_RANK, A_RANK, G_RANK = 64, 64, 128

LANES = 128
SUBLANES = 8
VMEM_LIMIT = 56 * 1024 * 1024

CHUNK = 64
WKV_SEQS_PER_STEP = 4
TM_INPROJ, TM_MERGE, TM_MLP = 512, 512, 512
ATTN_TQ, ATTN_TK = 256, 512
PAGES_PER_STEP = 32

NN = (((1,), (0,)), ((), ()))
NT = (((1,), (1,)), ((), ()))


def _cparams(*sem):
    return pltpu.CompilerParams(dimension_semantics=sem, vmem_limit_bytes=VMEM_LIMIT)


def _dot(a, b, dims=NN):
    return lax.dot_general(a.astype(BF16), b.astype(BF16), dims, preferred_element_type=F32)


def _split2(x):
    hi = x.astype(BF16)
    lo = (x - hi.astype(F32)).astype(BF16)
    return hi, lo


def _mm3(a, b, dims=NN):
    a_hi, a_lo = _split2(a)
    b_hi, b_lo = _split2(b)
    return lax.dot_general(jnp.concatenate([a_hi, a_lo, a_hi], axis=1),
                           jnp.concatenate([b_hi, b_hi, b_lo], axis=dims[0][1][0]), dims,
                           preferred_element_type=F32)


def _seg(x, e2):
    return jnp.dot(jnp.concatenate(_split2(x), axis=1), e2, preferred_element_type=F32)


def _sigmoid(x):
    return 0.5 * jnp.tanh(0.5 * x) + 0.5


def _rms_mod(x, g, sc, sh):
    y = x * lax.rsqrt(jnp.mean(x * x, axis=-1, keepdims=True) + NORM_EPS) * g
    return y * (1.0 + sc) + sh


def _ada_kernel(c_ref, w_ref, b_ref, o_ref):
    c = c_ref[...]
    o_ref[0] = _dot(c * _sigmoid(c), w_ref[0]) + b_ref[0]


def _ada_mod(c_all, ada_w, ada_b):
    depth, d, n = ada_w.shape
    rows = c_all.shape[0]
    tn = d
    return pl.pallas_call(
        _ada_kernel,
        grid=(depth, n // tn),
        in_specs=[
            pl.BlockSpec((rows, d), lambda l, j: (0, 0)),
            pl.BlockSpec((1, d, tn), lambda l, j: (l, 0, j)),
            pl.BlockSpec((1, 1, tn), lambda l, j: (l, 0, j)),
        ],
        out_specs=pl.BlockSpec((1, rows, tn), lambda l, j: (l, 0, j)),
        out_shape=jax.ShapeDtypeStruct((depth, rows, n), F32),
        compiler_params=_cparams("parallel", "parallel"),
        name="ada_mod",
    )(c_all, ada_w, ada_b.reshape(depth, 1, n))


def _inproj_kernel(shift_cols, qk_w, v_w, stacked, *refs):
    x_ref, sc_ref, sh_ref, g_ref, w_ref, qn_ref, kn_ref, e_ref, cos_ref, s1_ref, s2_ref = refs[-18:-7]
    zs_ref, q_ref, k_ref, kb_ref, v_ref, vb_ref, gate_ref = refs[-7:]
    hb = _rms_mod(x_ref[...], g_ref[...], sc_ref[0], sh_ref[0]).astype(BF16)
    o = 0
    zs_ref[...] = jnp.dot(hb, w_ref[:, o:o + shift_cols], preferred_element_type=F32)
    o += shift_cols
    reps = qk_w // LANES
    cos = jnp.concatenate([cos_ref[...]] * reps, axis=1)
    s1 = jnp.concatenate([s1_ref[...]] * reps, axis=1)
    s2 = jnp.concatenate([s2_ref[...]] * reps, axis=1)

    def norm_rope(z, gain):
        ms = _seg(z * z, e_ref[...]) * (1.0 / A_DK)
        y = z * lax.rsqrt(ms + NORM_EPS) * gain
        return y * cos + pltpu.roll(y, ROT_DIM // 2, 1) * s1 + pltpu.roll(y, qk_w - ROT_DIM // 2, 1) * s2

    q = norm_rope(jnp.dot(hb, w_ref[:, o:o + qk_w], preferred_element_type=F32), qn_ref[...])
    q_ref[...] = (q * (A_DK ** -0.5)).astype(BF16)
    o += qk_w
    k = norm_rope(jnp.dot(hb, w_ref[:, o:o + qk_w], preferred_element_type=F32), kn_ref[...])
    if stacked:
        k_ref[0, 0] = k.T
    else:
        k_ref[...] = k
    kb_ref[...] = k.astype(BF16)
    o += qk_w
    v = jnp.dot(hb, w_ref[:, o:o + v_w], preferred_element_type=F32)
    if stacked:
        for h in range(v_w // LANES):
            v_ref[0, pl.ds(h, v.shape[0], stride=v_w // LANES), :] = v[:, h * LANES:(h + 1) * LANES]
    else:
        v_ref[...] = v
    vb_ref[...] = v.astype(BF16)
    o += v_w
    gate_ref[...] = _sigmoid(jnp.dot(hb, w_ref[:, o:], preferred_element_type=F32))


def _inproj(x, mods, g, w_bf, qn, kn, e_qk, rope, shift_cols, qk_w, v_w, tm, tps, stack=None):
    rows, d = x.shape
    in_cols = w_bf.shape[1]
    gate_w = in_cols - shift_cols - 2 * qk_w - v_w
    sc, sh = mods
    r_mod = sc.shape[1]
    row = lambda i: (i, 0)
    const = lambda i: (0, 0)
    modmap = lambda i: (i // tps, 0, 0)
    ropemap = lambda i: (i % tps, 0)
    outs = [(shift_cols, F32), (qk_w, BF16), (qk_w, F32), (qk_w, BF16), (v_w, F32), (v_w, BF16), (gate_w, F32)]
    out_specs = [pl.BlockSpec((tm, w), row) for w, _ in outs]
    out_shape = [jax.ShapeDtypeStruct((rows, w), dt) for w, dt in outs]
    extra_in, extra_specs, aliases = [], [], {}
    if stack is not None:
        layer, depth, prev = stack
        heads = v_w // LANES
        seq_len = tm * tps
        out_specs[2] = pl.BlockSpec((1, 1, qk_w, tm), lambda i: (layer, i // tps, 0, i % tps))
        out_shape[2] = jax.ShapeDtypeStruct((depth, rows // seq_len, qk_w, seq_len), F32)
        out_specs[4] = pl.BlockSpec((1, tm * heads, LANES), lambda i: (layer, i, 0))
        out_shape[4] = jax.ShapeDtypeStruct((depth, rows * heads, LANES), F32)
        if prev is not None:
            extra_in = list(prev)
            extra_specs = [pl.BlockSpec(memory_space=pl.ANY)] * 2
            aliases = {0: 2, 1: 4}
    return pl.pallas_call(
        functools.partial(_inproj_kernel, shift_cols, qk_w, v_w, stack is not None),
        grid=(rows // tm,),
        input_output_aliases=aliases,
        in_specs=extra_specs + [
            pl.BlockSpec((tm, d), row),
            pl.BlockSpec((1, r_mod, d), modmap),
            pl.BlockSpec((1, r_mod, d), modmap),
            pl.BlockSpec((1, d), const),
            pl.BlockSpec((d, in_cols), const, pipeline_mode=pl.Buffered(1)),
            pl.BlockSpec((1, qk_w), const),
            pl.BlockSpec((1, qk_w), const),
            pl.BlockSpec((2 * qk_w, qk_w), const),
            pl.BlockSpec((tm, LANES), ropemap),
            pl.BlockSpec((tm, LANES), ropemap),
            pl.BlockSpec((tm, LANES), ropemap),
        ],
        out_specs=out_specs,
        out_shape=out_shape,
        compiler_params=_cparams("parallel"),
        name="inproj",
    )(*extra_in, x, sc, sh, g, w_bf, qn, kn, e_qk, *rope)


def _rwkv_kernel(has_vres, valid, *refs):
    (zs_ref, first_ref, mu_ref, w0_ref, wb_ref, a0_ref, ab_ref, gb_ref, kk_ref, ka_ref, e_ref, rk_ref, lnw_ref,
     lnb_ref, z0_ref) = refs[:15]
    if has_vres:
        v0_ref, va_ref, vb_ref, vf_ref, y_ref, z_ref, carry_ref = refs[15:]
    else:
        y_ref, z_ref, vout_ref, carry_ref = refs[15:]
    C = CHUNK
    P2 = 2 * C
    nb, _, cols = zs_ref.shape
    rw = rk_ref.shape[1]
    chunk = pl.program_id(1)

    @pl.when(chunk == 0)
    def _():
        z_ref[...] = z0_ref[...]
        for s in range(nb):
            carry_ref[s] = jnp.broadcast_to(first_ref[s], (SUBLANES, cols))

    def prep():
        row_id = lax.broadcasted_iota(jnp.int32, (C, 1), 0)
        zs, prevs = [], []
        for s in range(nb):
            z = zs_ref[s]
            prevs.append(jnp.where(row_id == 0, carry_ref[s, SUBLANES - 1:SUBLANES, :], pltpu.roll(z, 1, 0)))
            carry_ref[s] = z[C - SUBLANES:, :]
            zs.append(z)
        z = jnp.concatenate(zs, axis=0)
        xs = z + (jnp.concatenate(prevs, axis=0) - z) * mu_ref[...]
        r = xs[:, 0:rw]
        k = xs[:, rw:2 * rw]
        v = xs[:, 2 * rw:3 * rw]
        zwa = xs[:, 3 * rw:3 * rw + W_RANK + A_RANK]
        zg = xs[:, 3 * rw + W_RANK + A_RANK:]
        u = -(w0_ref[...] + _dot(jnp.tanh(zwa), wb_ref[...]))
        softplus = jnp.maximum(u, 0.0) + jnp.log(1.0 + jnp.exp(-jnp.abs(u)))
        lw = -jnp.exp(-softplus - 0.5)
        a = _sigmoid(a0_ref[...] + _dot(zwa, ab_ref[...]))
        g = _dot(_sigmoid(zg), gb_ref[...])
        if has_vres:
            vg = _sigmoid(v0_ref[...] + _dot(_dot(v, va_ref[...]), vb_ref[...]))
            v = v + (jnp.concatenate([vf_ref[s] for s in range(nb)], axis=0) - v) * vg
        else:
            for s in range(nb):
                vout_ref[s] = v[s * C:(s + 1) * C]
        kk = k * kk_ref[...]
        kk = kk / jnp.maximum(jnp.sqrt(_seg(kk * kk, e_ref[...])), 1e-12)
        k = k * (1.0 + (a - 1.0) * ka_ref[...])
        b = kk * a
        if valid < C:
            keep = (lax.broadcasted_iota(jnp.int32, (nb * C, 1), 0) & (C - 1)) < valid
            lw, k, v, kk, b = (jnp.where(keep, t, 0.0) for t in (lw, k, v, kk, b))
        return r, lw, k, v, kk, b, g

    ri = lax.broadcasted_iota(jnp.int32, (P2, P2), 0)
    ci = lax.broadcasted_iota(jnp.int32, (P2, P2), 1)
    strict = ci < ri
    incl = ci <= ri
    eye = (ci == ri).astype(F32)
    blk = lambda s: (ri >> s) == (ci >> s)
    e_pair = blk(6).astype(BF16)
    e_head = jnp.concatenate([e_pair, e_pair], axis=0)
    ti = lax.broadcasted_iota(jnp.int32, (C, C), 0)
    tj = lax.broadcasted_iota(jnp.int32, (C, C), 1)
    tril1 = (tj <= ti).astype(BF16)
    head_a = lax.broadcasted_iota(jnp.int32, (C, LANES), 1) < R_HEAD

    def stack(x):
        return jnp.concatenate([jnp.where(head_a, x, 0.0), jnp.where(head_a, 0.0, x)], axis=0)

    units = [(s, p) for s in range(nb) for p in range(rw // LANES)]
    pairs = range(len(units))
    sls = [slice(p * LANES, (p + 1) * LANES) for _, p in units]
    mm = _dot
    cat = jnp.concatenate
    tok = prep()
    ld = lambda x: [x[s * C:(s + 1) * C, sl] for (s, _), sl in zip(units, sls)]
    r, lw, k, v, kp, b, gate = (ld(x) for x in tok)

    def split3(x):
        x0 = x.astype(BF16)
        x1 = x - x0.astype(F32)
        x1b = x1.astype(BF16)
        x2b = (x1 - x1b.astype(F32)).astype(BF16)
        return cat([x0, x1b, x2b], axis=0)

    logp_all = jnp.dot(cat([tril1] * 3, axis=1), cat([split3(x) for x in lw], axis=1), preferred_element_type=F32)
    logp = [logp_all[:, p * LANES:(p + 1) * LANES] for p in pairs]
    logpc = [x[C - 1:C, :] for x in logp]
    e_neg = [jnp.exp(-x) for x in logp]
    e_end = [jnp.exp(xc - x) for xc, x in zip(logpc, logp)]
    rt = [stack(r[p] * jnp.exp(logp[p])) for p in pairs]
    kpt = [stack(kp[p] * jnp.exp(logp[p] - lw[p])) for p in pairs]
    v_s = [stack(v[p]) for p in pairs]
    gram = [_dot(cat([kpt[p], rt[p]], axis=0), cat([stack(b[p] * e_neg[p]), stack(k[p] * e_neg[p])], axis=0), NT)
            for p in pairs]
    lb = [jnp.where(strict, g[:P2, :P2], 0.0) for g in gram]
    lk = [jnp.where(strict, g[:P2, P2:], 0.0) for g in gram]
    mb = [jnp.where(incl, g[P2:, :P2], 0.0) for g in gram]
    mk = [jnp.where(incl, g[P2:, P2:], 0.0) for g in gram]
    x = [mm(lk[p], v_s[p]) for p in pairs]
    n1 = [jnp.where(blk(3), -m, 0.0) for m in lb]
    n2 = [mm(m, m) for m in n1]
    t = [mm(eye + a, eye + c) for a, c in zip(n1, n2)]
    n4 = [mm(m, m) for m in n2]
    t = [mm(a, eye + c) for a, c in zip(t, n4)]
    for s in (3, 4, 5):
        off_mask = blk(s + 1) & jnp.logical_not(blk(s))
        w = [mm(jnp.where(off_mask, lb[p], 0.0), t[p]) for p in pairs]
        t = [t[p] - mm(t[p], w[p]) for p in pairs]
    wu = [-mm(t[p], cat([kpt[p], x[p]], axis=1)) for p in pairs]
    lhs = [cat([cat([mb[p], mk[p]], axis=1),
                cat([stack(b[p] * e_end[p]).T, stack(k[p] * e_end[p]).T], axis=1)], axis=0) for p in pairs]
    rhs = [cat([wu[p], cat([jnp.zeros_like(v_s[p]), v_s[p]], axis=1)], axis=0) for p in pairs]
    big = [mm(lhs[p], rhs[p]) for p in pairs]
    qa = [cat([rt[p] + big[p][:P2, :LANES], eye * jnp.exp(logpc[p]) + big[p][P2:, :LANES]], axis=0) for p in pairs]
    qaz = [_mm3(qa[p], z_ref[units[p]]) for p in pairs]
    for p in pairs:
        z_ref[units[p]] = qaz[p][P2:] + big[p][P2:, LANES:]
    ys = [qaz[p][:P2] + big[p][:P2, LANES:] for p in pairs]
    y = cat([x[:C] + x[C:] for x in ys], axis=0)
    rkr = cat([r[p] * k[p] * rk_ref[:, sls[p]] for p in pairs], axis=0)
    sums = _seg(cat([y, rkr], axis=0), e_head)
    n_rows = y.shape[0]
    d = y - sums[:n_rows] * (1.0 / R_HEAD)
    inv_std = lax.rsqrt(_seg(d * d, e_head) * (1.0 / R_HEAD) + GN_EPS)
    for p in pairs:
        seq = units[p][0]
        rows = slice(p * C, (p + 1) * C)
        yn = d[rows] * inv_std[rows] * lnw_ref[:, sls[p]] + lnb_ref[:, sls[p]]
        bonus = sums[n_rows:][rows] * v[p]
        y_ref[seq, :, sls[p]] = ((yn + bonus) * gate[p]).astype(y_ref.dtype)


def _rwkv(zs, first, p, e_rw, rk, lnw, lnb, z0, v_first, valid, nb):
    bsz, t, cols = zs.shape
    rw = rk.shape[1]
    assert bsz % nb == 0 and t % CHUNK == 0 and rw % LANES == 0, (zs.shape, nb)
    n_pairs = rw // LANES
    has_vres = v_first is not None
    seq = lambda bb, c: (bb, c, 0)
    per_b = lambda bb, c: (bb, 0, 0)
    const = lambda bb, c: (0, 0)
    zmap = lambda bb, c: (bb, 0, 0, 0)
    full = lambda a: pl.BlockSpec(a.shape, const)
    params = [p["shift_mu"], p["rw_w0"], p["wb_pad"], p["rw_a0"], p["ab_pad"], p["rw_gb"], p["rw_kk"], p["rw_ka"],
              e_rw, rk, lnw, lnb]
    args = [zs, first] + params + [z0]
    in_specs = ([pl.BlockSpec((nb, CHUNK, cols), seq), pl.BlockSpec((nb, 1, cols), per_b)] + [full(a) for a in params]
                + [pl.BlockSpec((nb, n_pairs, LANES, LANES), zmap)])
    out_specs = [pl.BlockSpec((nb, CHUNK, rw), seq), pl.BlockSpec((nb, n_pairs, LANES, LANES), zmap)]
    out_shape = [jax.ShapeDtypeStruct((bsz, t, rw), BF16), jax.ShapeDtypeStruct((bsz, n_pairs, LANES, LANES), F32)]
    if has_vres:
        extra = [p["rw_v0"], p["rw_va"], p["rw_vb"]]
        args += extra + [v_first]
        in_specs += [full(a) for a in extra] + [pl.BlockSpec((nb, CHUNK, rw), seq)]
    else:
        out_specs.append(pl.BlockSpec((nb, CHUNK, rw), seq))
        out_shape.append(jax.ShapeDtypeStruct((bsz, t, rw), F32))
    return pl.pallas_call(
        functools.partial(_rwkv_kernel, has_vres, valid),
        grid=(bsz // nb, t // CHUNK),
        in_specs=in_specs,
        out_specs=out_specs,
        out_shape=out_shape,
        scratch_shapes=[pltpu.VMEM((nb, SUBLANES, cols), F32)],
        compiler_params=_cparams("parallel", "arbitrary"),
        name="rwkv",
    )(*args)


def _state_to_blockdiag(s):
    bsz, h, n, _ = s.shape
    zt = jnp.swapaxes(s, -1, -2).reshape(bsz, h // 2, 2, n, n)
    zero = jnp.zeros_like(zt[:, :, 0])
    top = jnp.concatenate([zt[:, :, 0], zero], axis=-1)
    bot = jnp.concatenate([zero, zt[:, :, 1]], axis=-1)
    return jnp.concatenate([top, bot], axis=-2)


def _blockdiag_to_state(z):
    n = R_HEAD
    za = z[:, :, :n, :n]
    zb = z[:, :, n:, n:]
    s = jnp.stack([za, zb], axis=2)
    s = s.reshape(z.shape[0], -1, n, n)
    return jnp.swapaxes(s, -1, -2)


def _attn_kernel(scale_out, tk, *refs):
    _attn_body(scale_out, tk, pl.program_id(1), *refs)


def _attn_body(scale_out, tk, qi, lam_ref, q_ref, k_ref, v_ref, sg_ref, o_ref, m_ref, l_ref, acc_ref):
    tq = q_ref.shape[1]
    heads = range(q_ref.shape[2] // LANES)
    hsl = [slice(h * LANES, (h + 1) * LANES) for h in heads]
    jd = qi // (tk // tq)
    first = lax.broadcasted_iota(jnp.int32, (tq, LANES), 1) < A_DK
    q2 = []
    for h in heads:
        q = q_ref[0, :, hsl[h]]
        zero = jnp.zeros_like(q)
        q2.append(jnp.concatenate([jnp.where(first, q, zero), jnp.where(first, zero, q)], axis=0))
    m_ref[...] = jnp.full(m_ref.shape, -jnp.inf, F32)
    l_ref[...] = jnp.zeros(l_ref.shape, F32)
    acc_ref[...] = jnp.zeros(acc_ref.shape, F32)
    reps = tk // LANES

    def block(start, mask):
        s = [lax.dot_general(q2[h], k_ref[0, pl.ds(start, tk), hsl[h]], NT, preferred_element_type=F32)
             for h in heads]
        if mask is not None:
            s = [jnp.where(mask, x, -jnp.inf) for x in s]
        m_prev = [m_ref[h] for h in heads]
        m_next = [jnp.maximum(m_prev[h], jnp.max(s[h], axis=1, keepdims=True)) for h in heads]
        p = [jnp.exp(s[h] - jnp.concatenate([m_next[h]] * reps, axis=1)) for h in heads]
        alpha = [jnp.exp(m_prev[h] - m_next[h]) for h in heads]
        for h in heads:
            l_ref[h] = alpha[h] * l_ref[h] + jnp.sum(p[h], axis=1, keepdims=True)
            m_ref[h] = m_next[h]
        pv = [jnp.dot(p[h].astype(BF16), v_ref[0, pl.ds(start, tk), hsl[h]], preferred_element_type=F32)
              for h in heads]
        for h in heads:
            acc_ref[h] = alpha[h] * acc_ref[h] + pv[h]

    def body(j, carry):
        block(pl.multiple_of(j * tk, tk), None)
        return carry

    lax.fori_loop(0, jd, body, 0)
    row = lax.broadcasted_iota(jnp.int32, (2 * tq, tk), 0)
    col = lax.broadcasted_iota(jnp.int32, (2 * tq, tk), 1)
    row = jnp.where(row >= tq, row - tq, row) + (qi * tq - jd * tk)
    block(pl.multiple_of(jd * tk, tk), col <= row)
    for h in heads:
        o = acc_ref[h] / l_ref[h]
        att = o[:tq] - lam_ref[0] * o[tq:]
        att = att * lax.rsqrt(jnp.mean(att * att, axis=-1, keepdims=True) + NORM_EPS) * sg_ref[...]
        o_ref[0, :, hsl[h]] = (att * scale_out).astype(o_ref.dtype)


def _attn_prompt(q, k, v, lam, subln, scale_out, tq, tk):
    bsz, t, w = q.shape
    heads = w // LANES
    return pl.pallas_call(
        functools.partial(_attn_kernel, scale_out, tk),
        grid=(bsz, t // tq),
        in_specs=[
            pl.BlockSpec(memory_space=pltpu.SMEM),
            pl.BlockSpec((1, tq, w), lambda b, i: (b, i, 0)),
            pl.BlockSpec((1, t, w), lambda b, i: (b, 0, 0)),
            pl.BlockSpec((1, t, w), lambda b, i: (b, 0, 0)),
            pl.BlockSpec((1, LANES), lambda b, i: (0, 0)),
        ],
        out_specs=pl.BlockSpec((1, tq, w), lambda b, i: (b, i, 0)),
        out_shape=jax.ShapeDtypeStruct((bsz, t, w), BF16),
        scratch_shapes=[pltpu.VMEM((heads, 2 * tq, LANES), F32)] * 3,
        compiler_params=_cparams("parallel", "arbitrary"),
        name="attn_prompt",
    )(lam, q, k, v, subln)


def _decode_kernel(scale_out, n_pages_step, pt_ref, *refs):
    _decode_body(scale_out, n_pages_step, pl.program_id(1), pl.num_programs(1), *refs)


def _decode_body(scale_out, n_pages_step, j, n_steps, lam_ref, q_ref, kn_ref, vn_ref, *refs):
    k_refs = refs[:n_pages_step]
    v_refs = refs[n_pages_step:2 * n_pages_step]
    sg_ref, o_ref, m_ref, l_ref, acc_ref = refs[2 * n_pages_step:]
    rows = q_ref.shape[1]
    hrows = rows // A_HEADS
    tpad = hrows // 2

    @pl.when(j == 0)
    def _():
        m_ref[...] = jnp.full(m_ref.shape, -jnp.inf, F32)
        l_ref[...] = jnp.zeros(l_ref.shape, F32)
        acc_ref[...] = jnp.zeros(acc_ref.shape, F32)

    q = q_ref[0]

    def update(s_list, v_fn):
        m_prev = m_ref[...]
        m_cur = s_list[0].max(axis=1, keepdims=True)
        for s in s_list[1:]:
            m_cur = jnp.maximum(m_cur, s.max(axis=1, keepdims=True))
        m_next = jnp.maximum(m_prev, m_cur)
        alpha = jnp.exp(m_prev - m_next)
        l_new = alpha * l_ref[...]
        p_list = []
        for s in s_list:
            p = jnp.exp(s - m_next)
            l_new = l_new + jnp.sum(p, axis=1, keepdims=True)
            p_list.append(p.astype(BF16))
        l_ref[...] = l_new
        m_ref[...] = m_next
        for h in range(A_HEADS):
            hs = slice(h * hrows, (h + 1) * hrows)
            acc = alpha[hs] * acc_ref[hs, :]
            for i, p in enumerate(p_list):
                acc = acc + jnp.dot(p[hs], v_fn(i, h).astype(BF16), preferred_element_type=F32)
            acc_ref[hs, :] = acc

    s_list = [jnp.dot(q, kr[0].astype(BF16), preferred_element_type=F32) for kr in k_refs]
    update(s_list, lambda i, h: v_refs[i][0, pl.ds(h, PAGE_SIZE, stride=A_HEADS), :])

    @pl.when(j == n_steps - 1)
    def _():
        s = jnp.dot(q, kn_ref[0].astype(BF16), preferred_element_type=F32)
        tok = lax.broadcasted_iota(jnp.int32, s.shape, 0) & (tpad - 1)
        key = lax.broadcasted_iota(jnp.int32, s.shape, 1)
        update([jnp.where(key <= tok, s, -jnp.inf)], lambda i, h: vn_ref[0, :, h * LANES:(h + 1) * LANES])
        inv_l = 1.0 / l_ref[...]
        for h in range(A_HEADS):
            r0 = slice(h * hrows, h * hrows + tpad)
            r1 = slice(h * hrows + tpad, (h + 1) * hrows)
            att = acc_ref[r0, :] * inv_l[r0] - lam_ref[0] * (acc_ref[r1, :] * inv_l[r1])
            att = att * lax.rsqrt(jnp.mean(att * att, axis=-1, keepdims=True) + NORM_EPS) * sg_ref[...]
            o_ref[0, :, h * LANES:(h + 1) * LANES] = (att * scale_out).astype(o_ref.dtype)


def _attn_decode(qbig, kt_new, v_new, cache_kt, cache_v, page_ids, lam, subln, scale_out):
    bsz, rows, w = qbig.shape
    n_pages = page_ids.shape[1]
    pps = PAGES_PER_STEP
    tpad = rows // (2 * A_HEADS)

    def page_spec(i):
        return pl.BlockSpec((1, w, PAGE_SIZE), lambda b, j, pt: (pt[b, j * pps + i], 0, 0))

    per_b = lambda b, j, pt: (b, 0, 0)
    grid_spec = pltpu.PrefetchScalarGridSpec(
        num_scalar_prefetch=1,
        grid=(bsz, n_pages // pps),
        in_specs=[
            pl.BlockSpec(memory_space=pltpu.SMEM),
            pl.BlockSpec((1, rows, w), per_b),
            pl.BlockSpec((1, w, PAGE_SIZE), per_b),
            pl.BlockSpec((1, PAGE_SIZE, w), per_b),
        ] + [page_spec(i) for i in range(pps)] + [page_spec(i) for i in range(pps)]
        + [pl.BlockSpec((1, LANES), lambda b, j, pt: (0, 0))],
        out_specs=pl.BlockSpec((1, tpad, w), per_b),
        scratch_shapes=[pltpu.VMEM((rows, LANES), F32)] * 3,
    )
    return pl.pallas_call(
        functools.partial(_decode_kernel, scale_out, pps),
        grid_spec=grid_spec,
        out_shape=jax.ShapeDtypeStruct((bsz, tpad, w), BF16),
        compiler_params=_cparams("parallel", "arbitrary"),
        name="attn_decode",
    )(page_ids, lam, qbig, kt_new, v_new, *([cache_kt] * pps), *([cache_v] * pps), subln)


def _attn_both_kernel(scale_out, tk, pps, steps_per_seq, pt_ref, lam_ref, q_ref, k_ref, v_ref, sg_ref, qd_ref, kn_ref,
                      vn_ref, *refs):
    page_refs = refs[:2 * pps]
    o_ref, od_ref, m_ref, l_ref, acc_ref, md_ref, ld_ref, accd_ref = refs[2 * pps:]
    qi = pl.program_id(1)
    _attn_body(scale_out, tk, qi, lam_ref, q_ref, k_ref, v_ref, sg_ref, o_ref, m_ref, l_ref, acc_ref)
    step = pl.program_id(0) * pl.num_programs(1) + qi
    _decode_body(scale_out, pps, step % steps_per_seq, steps_per_seq, lam_ref, qd_ref, kn_ref, vn_ref, *page_refs,
                 sg_ref, od_ref, md_ref, ld_ref, accd_ref)


def _attention(q, k, v, qbig, kt_new, v_new, cache_kt, cache_v, page_ids, lam, subln, scale_out):
    bsz, t, w = q.shape
    bs, rows, _ = qbig.shape
    tq, tk, pps = ATTN_TQ, ATTN_TK, PAGES_PER_STEP
    nq, spb = t // tq, page_ids.shape[1] // pps
    if bsz * nq != bs * spb:
        return (_attn_prompt(q, k, v, lam, subln, scale_out, tq, tk),
                _attn_decode(qbig, kt_new, v_new, cache_kt, cache_v, page_ids, lam, subln, scale_out))
    heads = w // LANES
    tpad = rows // (2 * A_HEADS)
    seq = lambda b, i, pt: ((b * nq + i) // spb, 0, 0)

    def page_spec(n):
        return pl.BlockSpec((1, w, PAGE_SIZE), lambda b, i, pt: (pt[b * nq + i, n], 0, 0))

    grid_spec = pltpu.PrefetchScalarGridSpec(
        num_scalar_prefetch=1,
        grid=(bsz, nq),
        in_specs=[
            pl.BlockSpec(memory_space=pltpu.SMEM),
            pl.BlockSpec((1, tq, w), lambda b, i, pt: (b, i, 0)),
            pl.BlockSpec((1, t, w), lambda b, i, pt: (b, 0, 0), pipeline_mode=pl.Buffered(1)),
            pl.BlockSpec((1, t, w), lambda b, i, pt: (b, 0, 0), pipeline_mode=pl.Buffered(1)),
            pl.BlockSpec((1, LANES), lambda b, i, pt: (0, 0)),
            pl.BlockSpec((1, rows, w), seq),
            pl.BlockSpec((1, w, PAGE_SIZE), seq),
            pl.BlockSpec((1, PAGE_SIZE, w), seq),
        ] + [page_spec(n) for n in range(pps)] + [page_spec(n) for n in range(pps)],
        out_specs=[pl.BlockSpec((1, tq, w), lambda b, i, pt: (b, i, 0)), pl.BlockSpec((1, tpad, w), seq)],
        scratch_shapes=[pltpu.VMEM((heads, 2 * tq, LANES), F32)] * 3 + [pltpu.VMEM((rows, LANES), F32)] * 3,
    )
    return pl.pallas_call(
        functools.partial(_attn_both_kernel, scale_out, tk, pps, spb),
        grid_spec=grid_spec,
        out_shape=[jax.ShapeDtypeStruct((bsz, t, w), BF16), jax.ShapeDtypeStruct((bs, tpad, w), BF16)],
        compiler_params=_cparams("arbitrary", "arbitrary"),
        name="attn_both",
    )(page_ids.reshape(bs * spb, pps), lam, q, k, v, subln, qbig, kt_new, v_new, *([cache_kt] * pps),
      *([cache_v] * pps))


def _merge_kernel(rw_ref, att_ref, gate_ref, x_ref, gt_ref, brw_ref, bra_ref, wo_ref, o_ref):
    d = x_ref.shape[1]
    m = (gate_ref[:, :d] * jnp.dot(rw_ref[...], brw_ref[...], preferred_element_type=F32)
         + gate_ref[:, d:] * jnp.dot(att_ref[...], bra_ref[...], preferred_element_type=F32))
    o_ref[...] = x_ref[...] + gt_ref[0] * jnp.dot(m.astype(BF16), wo_ref[...], preferred_element_type=F32)


def _merge(rw, att, gates, x, gt, brw, bra, wo, tm, tps):
    rows, d = x.shape
    row = lambda i: (i, 0)
    const = lambda i: (0, 0)
    return pl.pallas_call(
        _merge_kernel,
        grid=(rows // tm,),
        in_specs=[
            pl.BlockSpec((tm, rw.shape[1]), row),
            pl.BlockSpec((tm, att.shape[1]), row),
            pl.BlockSpec((tm, gates.shape[1]), row),
            pl.BlockSpec((tm, d), row),
            pl.BlockSpec((1, gt.shape[1], d), lambda i: (i // tps, 0, 0)),
            pl.BlockSpec(brw.shape, const),
            pl.BlockSpec(bra.shape, const),
            pl.BlockSpec(wo.shape, const),
        ],
        out_specs=pl.BlockSpec((tm, d), row),
        out_shape=jax.ShapeDtypeStruct((rows, d), F32),
        compiler_params=_cparams("parallel"),
        name="merge_out",
    )(rw, att, gates, x, gt, brw, bra, wo)


def _mlp_kernel(fc, x_ref, sc_ref, sh_ref, gt_ref, g_ref, w1_ref, w2_ref, o_ref):
    x = x_ref[...]
    hb = _rms_mod(x, g_ref[...], sc_ref[0], sh_ref[0]).astype(BF16)
    acc = jnp.zeros(x.shape, F32)
    for c in range(w1_ref.shape[1] // fc):
        u = jnp.dot(hb, w1_ref[:, c * fc:(c + 1) * fc], preferred_element_type=F32)
        u = jnp.square(jnp.maximum(u, 0.0)).astype(BF16)
        acc = acc + jnp.dot(u, w2_ref[c * fc:(c + 1) * fc, :], preferred_element_type=F32)
    o_ref[...] = x + gt_ref[0] * acc


def _mlp(x, mods, g, w1, w2, tm, tps):
    rows, d = x.shape
    sc, sh, gt = mods
    r_mod = sc.shape[1]
    row = lambda i: (i, 0)
    const = lambda i: (0, 0)
    modspec = pl.BlockSpec((1, r_mod, d), lambda i: (i // tps, 0, 0))
    return pl.pallas_call(
        functools.partial(_mlp_kernel, 1024),
        grid=(rows // tm,),
        in_specs=[pl.BlockSpec((tm, d), row), modspec, modspec, modspec, pl.BlockSpec((1, d), const),
                  pl.BlockSpec(w1.shape, const, pipeline_mode=pl.Buffered(1)),
                  pl.BlockSpec(w2.shape, const, pipeline_mode=pl.Buffered(1))],
        out_specs=pl.BlockSpec((tm, d), row),
        out_shape=jax.ShapeDtypeStruct((rows, d), F32),
        compiler_params=_cparams("parallel"),
        name="mlp",
    )(x, sc, sh, gt, g, w1, w2)


def _tail_kernel(fc, rw_ref, att_ref, gate_ref, x_ref, gt1_ref, brw_ref, bra_ref, wo_ref, sc_ref, sh_ref, gt2_ref,
                 g_ref, w1_ref, w2_ref, o_ref):
    d = x_ref.shape[1]
    m = (gate_ref[:, :d] * jnp.dot(rw_ref[...], brw_ref[...], preferred_element_type=F32)
         + gate_ref[:, d:] * jnp.dot(att_ref[...], bra_ref[...], preferred_element_type=F32))
    x = x_ref[...] + gt1_ref[0] * jnp.dot(m.astype(BF16), wo_ref[...], preferred_element_type=F32)
    hb = _rms_mod(x, g_ref[...], sc_ref[0], sh_ref[0]).astype(BF16)
    acc = jnp.zeros(x.shape, F32)
    for c in range(w1_ref.shape[1] // fc):
        u = jnp.dot(hb, w1_ref[:, c * fc:(c + 1) * fc], preferred_element_type=F32)
        u = jnp.square(jnp.maximum(u, 0.0)).astype(BF16)
        acc = acc + jnp.dot(u, w2_ref[c * fc:(c + 1) * fc, :], preferred_element_type=F32)
    o_ref[...] = x + gt2_ref[0] * acc


def _tail(rw, att, gates, x, gt1, brw, bra, wo, mods, g, w1, w2, tm, tps):
    rows, d = x.shape
    sc, sh, gt2 = mods
    row = lambda i: (i, 0)
    const = lambda i: (0, 0)
    modspec = pl.BlockSpec((1, sc.shape[1], d), lambda i: (i // tps, 0, 0))
    single = lambda a: pl.BlockSpec(a.shape, const, pipeline_mode=pl.Buffered(1))
    return pl.pallas_call(
        functools.partial(_tail_kernel, 1024),
        grid=(rows // tm,),
        in_specs=[pl.BlockSpec((tm, rw.shape[1]), row), pl.BlockSpec((tm, att.shape[1]), row),
                  pl.BlockSpec((tm, gates.shape[1]), row), pl.BlockSpec((tm, d), row), modspec,
                  single(brw), single(bra), single(wo), modspec, modspec, modspec, pl.BlockSpec((1, d), const),
                  single(w1), single(w2)],
        out_specs=pl.BlockSpec((tm, d), row),
        out_shape=jax.ShapeDtypeStruct((rows, d), F32),
        compiler_params=_cparams("parallel"),
        name="tail",
    )(rw, att, gates, x, gt1, brw, bra, wo, sc, sh, gt2, g, w1, w2)


def _rope_tables(pos):
    half = ROT_DIM // 2
    inv = ROPE_THETA ** (-jnp.arange(half, dtype=F32) / half)
    ang = pos.astype(F32)[:, None] * inv[None, :]
    cos, sin = jnp.cos(ang), jnp.sin(ang)
    t = pos.shape[0]
    ones = jnp.ones((t, A_DK - ROT_DIM), F32)
    zeros_h = jnp.zeros((t, half), F32)
    zeros_r = jnp.zeros((t, A_DK - ROT_DIM), F32)
    c = jnp.concatenate([cos, cos, ones], axis=1)
    s1 = jnp.concatenate([zeros_h, sin, zeros_r], axis=1)
    s2 = jnp.concatenate([-sin, zeros_h, zeros_r], axis=1)
    rep = lambda a: jnp.concatenate([a] * (LANES // A_DK), axis=1)
    return rep(c), rep(s1), rep(s2)


def _block_ones2(n, blk):
    i = jnp.arange(n) // blk
    e = (i[:, None] == i[None, :]).astype(BF16)
    return jnp.concatenate([e, e], axis=0)


def _split_mod(mod, per_row_reps):
    parts = jnp.split(mod, 6, axis=-1)
    if per_row_reps is None:
        return [m[:, None, :] for m in parts]
    return [jnp.repeat(m, per_row_reps, axis=0)[None] for m in parts]


def kernel(x_prompt, x_sample, c_prompt, c_sample, cache_k, cache_v, state_wkv, state_shift, page_table, norm1_g,
           norm2_g, ada_w, ada_b, w_in, shift_mu, rw_w0, rw_wb, rw_a0, rw_ab, rw_gb, rw_v0, rw_va, rw_vb, rw_kk,
           rw_ka, rw_rk, rw_ln_w, rw_ln_b, at_qnorm, at_knorm, at_lq1, at_lk1, at_lq2, at_lk2, at_subln, br_rwkv,
           br_attn, w_out, mlp_w1, mlp_w2):
    bp, tp, d = x_prompt.shape
    bs, ts, _ = x_sample.shape
    depth = w_in.shape[0]
    rw = rw_w0.shape[1]
    shift_cols = shift_mu.shape[1]
    qk_w = A_HEADS * 2 * A_DK
    v_w = A_HEADS * 2 * A_DK
    n_phys = cache_k.shape[1]
    n_pages = page_table.shape[1]
    past_len = n_pages * PAGE_SIZE
    r_heads = rw // R_HEAD

    mod_all = _ada_mod(jnp.concatenate([c_prompt, c_sample], axis=0), ada_w, ada_b)
    rope_p = _rope_tables(jnp.arange(tp))
    rope_s = tuple(jnp.tile(a, (bs, 1)) for a in _rope_tables(past_len + jnp.arange(ts)))
    e_qk = _block_ones2(qk_w, A_DK)
    e_rw = _block_ones2(rw, R_HEAD)
    ckt = jnp.transpose(cache_k, (0, 1, 3, 4, 5, 2)).reshape(depth * n_phys, qk_w, PAGE_SIZE)
    cv = cache_v.reshape(depth * n_phys, PAGE_SIZE * A_HEADS, v_w // A_HEADS)
    tpad = SUBLANES
    lane = jnp.arange(qk_w)
    q_own = ((lane // LANES)[None, None, :] == jnp.arange(A_HEADS)[:, None, None]) & (
        ((lane // A_DK) % 2)[None, None, :] == jnp.arange(2)[None, :, None])

    xp = x_prompt.reshape(bp * tp, d)
    xs = x_sample.reshape(bs * ts, d)
    rows_s = bs * ts
    outs = {n: [] for n in ("ks", "vs", "sp", "ss", "shp", "shs")}
    vfp = vfs = None
    for l in range(depth):
        lam_init = 0.8 - 0.6 * math.exp(-0.3 * l)
        lam = (jnp.exp(jnp.sum(at_lq1[l] * at_lk1[l]).astype(F32)) - jnp.exp(jnp.sum(at_lq2[l] * at_lk2[l]).astype(F32))
               + lam_init).reshape(1)
        pad_w = jnp.zeros((A_RANK, rw), F32)
        p = dict(
            shift_mu=shift_mu[l][None], rw_w0=rw_w0[l][None], rw_a0=rw_a0[l][None],
            wb_pad=jnp.concatenate([rw_wb[l], pad_w], axis=0).astype(BF16),
            ab_pad=jnp.concatenate([jnp.zeros((W_RANK, rw), F32), rw_ab[l]], axis=0).astype(BF16),
            rw_gb=rw_gb[l].astype(BF16), rw_kk=rw_kk[l][None], rw_ka=rw_ka[l][None])
        if l > 0:
            p.update(rw_v0=rw_v0[l - 1][None], rw_va=rw_va[l - 1].astype(BF16), rw_vb=rw_vb[l - 1].astype(BF16))
        w_in_b = w_in[l].astype(BF16)
        qn = jnp.tile(at_qnorm[l], qk_w // A_DK)[None]
        kn = jnp.tile(at_knorm[l], qk_w // A_DK)[None]
        subln = at_subln[l][None]
        rk = rw_rk[l].reshape(1, rw)
        lnw, lnb = rw_ln_w[l][None], rw_ln_b[l][None]
        brw, bra, wo = br_rwkv[l].astype(BF16), br_attn[l].astype(BF16), w_out[l].astype(BF16)
        w1, w2 = mlp_w1[l].astype(BF16), mlp_w2[l].astype(BF16)
        g1, g2 = norm1_g[l][None], norm2_g[l][None]
        scale_out = 1.0 - lam_init

        sh1, sc1, gt1, sh2, sc2, gt2 = _split_mod(mod_all[l, :bp], None)
        zs, q, kt_all, kb, v_all, vb, gates = _inproj(
            xp, (sc1, sh1), g1, w_in_b, qn, kn, e_qk, rope_p, shift_cols, qk_w, v_w, TM_INPROJ, tp // TM_INPROJ,
            stack=(l, depth, None if l == 0 else (kt_all, v_all)))
        sh1s, sc1s, gt1s, sh2s, sc2s, gt2s = _split_mod(mod_all[l, bp:], ts)
        zs_s, q_s, k_s, _, v_s, _, gates_s = _inproj(xs, (sc1s, sh1s), g1, w_in_b, qn, kn, e_qk, rope_s, shift_cols, qk_w,
                                                    v_w, rows_s, 1)
        q3 = jnp.pad(q_s.reshape(bs, ts, qk_w), ((0, 0), (0, tpad - ts), (0, 0)))
        qbig = jnp.where(q_own[None, :, :, None, :], q3[:, None, None], 0).reshape(bs, 2 * A_HEADS * tpad, qk_w)
        pad_new = lambda a: jnp.pad(a.reshape(bs, ts, -1), ((0, 0), (0, PAGE_SIZE - ts), (0, 0)))
        att, att_s = _attention(q.reshape(bp, tp, qk_w), kb.reshape(bp, tp, qk_w), vb.reshape(bp, tp, v_w), qbig,
                                jnp.swapaxes(pad_new(k_s), 1, 2), pad_new(v_s), ckt, cv, page_table + l * n_phys, lam,
                                subln, scale_out)
        att = att.reshape(bp * tp, v_w)
        res = _rwkv(zs.reshape(bp, tp, shift_cols), jnp.zeros((bp, 1, shift_cols), F32), p, e_rw, rk, lnw, lnb,
                    jnp.zeros((bp, r_heads // 2, LANES, LANES), F32), vfp, CHUNK, WKV_SEQS_PER_STEP)
        rwo, zfin = res[:2]
        if l == 0:
            vfp = res[2]
        xp = _tail(rwo.reshape(bp * tp, rw), att, gates, xp, gt1, brw, bra, wo, (sc2, sh2, gt2), g2, w1, w2, TM_MLP,
                   tp // TM_MLP)
        outs["sp"].append(_blockdiag_to_state(zfin))
        outs["shp"].append(zs.reshape(bp, tp, shift_cols)[:, -1])

        att = att_s[:, :ts].reshape(rows_s, v_w)
        zs3 = zs_s.reshape(bs, ts, shift_cols)
        res = _rwkv(jnp.pad(zs3, ((0, 0), (0, CHUNK - ts), (0, 0))), state_shift[l][:, None, :], p, e_rw, rk, lnw, lnb,
                    _state_to_blockdiag(state_wkv[l]), vfs, ts, WKV_SEQS_PER_STEP)
        rwo, zfin = res[:2]
        if l == 0:
            vfs = res[2]
        rwo = rwo[:, :ts].reshape(rows_s, rw)
        xs = _tail(rwo, att, gates_s, xs, gt1s, brw, bra, wo, (sc2s, sh2s, gt2s), g2, w1, w2, rows_s, 1)
        outs["ks"].append(k_s.reshape(bs, ts, A_HEADS, 2, A_DK))
        outs["vs"].append(v_s.reshape(bs, ts, A_HEADS, 2 * A_DK))
        outs["ss"].append(_blockdiag_to_state(zfin))
        outs["shs"].append(zs3[:, -1])

    st = lambda n: jnp.stack(outs[n])
    new_k_prompt = jnp.transpose(kt_all.reshape(depth, bp, A_HEADS, 2, A_DK, tp), (0, 1, 5, 2, 3, 4))
    new_v_prompt = v_all.reshape(depth, bp, tp, A_HEADS, 2 * A_DK)
    return (xp.reshape(bp, tp, d), xs.reshape(bs, ts, d), new_k_prompt, new_v_prompt, st("ks"), st("vs"), st("sp"), st("ss"),
            st("shp"), st("shs"))
```
